```python
import jax, jax.numpy as jnp
from jax import lax
import numpy as np

D_MODEL = 1024
BATCH = 4
SEQ = 4096
DEPTH = 2

SSM_HEADS = 32
SSM_HEAD_DIM = 64
D_INNER = SSM_HEADS * SSM_HEAD_DIM
SSM_GROUPS = 4
D_STATE = 128
CONV_WIDTH = 4
CHUNK = 128
CONV_DIM = D_INNER + 2 * SSM_GROUPS * D_STATE
ATT_HEADS = 8
ATT_KV_HEADS = 2
ATT_HEAD_DIM = 128
ATT_WIDTH = ATT_HEADS * ATT_HEAD_DIM
IDX_HEADS = 8
IDX_DIM = 64
TOPK_MAX = 256
Q_BLOCK = 128
PEER_HEADS = 8
N_KEYS = 128
N_EXPERTS = N_KEYS * N_KEYS
PEER_TOPK = 16
PEER_QDIM = 256
PEER_BLOCK = 64
PLE_DIM = 256
ROPE_THETA = 10000.0
EPS = 1e-6

SPLITS = [D_INNER, CONV_DIM, SSM_HEADS, ATT_WIDTH, ATT_KV_HEADS * ATT_HEAD_DIM,
          ATT_KV_HEADS * ATT_HEAD_DIM, IDX_HEADS * IDX_DIM, IDX_DIM, IDX_HEADS, 2 * D_MODEL]

kernel_name = "hybrid_ssd_dsa_peer_block"


def _split_points():
    pts, acc = [], 0
    for s in SPLITS[:-1]:
        acc += s
        pts.append(acc)
    return pts


def rms_norm(x, g):
    xf = x.astype(jnp.float32)
    y = xf * lax.rsqrt(jnp.mean(xf * xf, axis=-1, keepdims=True) + EPS)
    return (y * g.astype(jnp.float32)).astype(x.dtype)


def rope_tables(positions, dim):
    inv = 1.0 / (ROPE_THETA ** (jnp.arange(0, dim, 2, dtype=jnp.float32) / dim))
    ang = positions.astype(jnp.float32)[..., None] * inv
    return jnp.cos(ang), jnp.sin(ang)


def apply_rope(x, cos, sin):
    x1, x2 = jnp.split(x.astype(jnp.float32), 2, axis=-1)
    c = cos[:, :, None, :]
    s = sin[:, :, None, :]
    return jnp.concatenate([x1 * c - x2 * s, x1 * s + x2 * c], axis=-1).astype(x.dtype)


def causal_depthwise_conv(x, w, b):
    c = x.shape[-1]
    rhs = jnp.transpose(w)[:, None, :].astype(x.dtype)
    y = lax.conv_general_dilated(x, rhs, window_strides=(1,),
                                 padding=[(CONV_WIDTH - 1, 0)],
                                 dimension_numbers=('NWC', 'WIO', 'NWC'),
                                 feature_group_count=c)
    return y + b.astype(x.dtype)


def ssd_chunked(x, dt, A, B, C):
    b, l, h, p = x.shape
    g, n = B.shape[2], B.shape[3]
    r = h // g
    c = l // CHUNK
    X = (x * dt[..., None]).reshape(b, c, CHUNK, g, r, p)
    Adt = (dt * A).reshape(b, c, CHUNK, g, r)
    Bc = B.astype(jnp.float32).reshape(b, c, CHUNK, g, n)
    Cc = C.astype(jnp.float32).reshape(b, c, CHUNK, g, n)
    cs = jnp.moveaxis(jnp.cumsum(Adt, axis=2), 2, -1)
    causal = jnp.tril(jnp.ones((CHUNK, CHUNK), dtype=bool))
    Lmat = jnp.exp(jnp.where(causal, cs[..., :, None] - cs[..., None, :], -jnp.inf))
    CB = jnp.einsum('bclgn,bcsgn->bcgls', Cc, Bc)
    y_diag = jnp.einsum('bcgls,bcgrls,bcsgrp->bclgrp', CB, Lmat, X)
    decay = jnp.exp(cs[..., -1:] - cs)
    states = jnp.einsum('bclgn,bcgrl,bclgrp->bcgrpn', Bc, decay, X)
    chunk_decay = jnp.exp(cs[..., -1])

    def step(carry, inp):
        st, dec = inp
        return carry * dec[..., None, None] + st, carry

    init = jnp.zeros((b, g, r, p, n), dtype=X.dtype)
    _, prev = lax.scan(step, init, (jnp.moveaxis(states, 1, 0), jnp.moveaxis(chunk_decay, 1, 0)))
    prev = jnp.moveaxis(prev, 0, 1)
    y_off = jnp.einsum('bclgn,bcgrpn,bcgrl->bclgrp', Cc, prev, jnp.exp(cs))
    return (y_diag + y_off).reshape(b, l, h, p)


def ssm_branch(z, xbc, dt_raw, conv_w, conv_b, dt_bias, a_log, d_skip, norm_g):
    b, l, _ = z.shape
    xbc = jax.nn.silu(causal_depthwise_conv(xbc, conv_w, conv_b))
    xs, Bm, Cm = jnp.split(xbc, [D_INNER, D_INNER + SSM_GROUPS * D_STATE], axis=-1)
    xs = xs.reshape(b, l, SSM_HEADS, SSM_HEAD_DIM).astype(jnp.float32)
    Bm = Bm.reshape(b, l, SSM_GROUPS, D_STATE)
    Cm = Cm.reshape(b, l, SSM_GROUPS, D_STATE)
    dt = jax.nn.softplus(dt_raw.astype(jnp.float32) + dt_bias.astype(jnp.float32))
    A = -jnp.exp(a_log.astype(jnp.float32))
    y = ssd_chunked(xs, dt, A, Bm, Cm) + d_skip.astype(jnp.float32)[:, None] * xs
    y = y.reshape(b, l, D_INNER) * jax.nn.silu(z.astype(jnp.float32))
    yg = y.reshape(b, l, SSM_GROUPS, D_INNER // SSM_GROUPS)
    yg = yg * lax.rsqrt(jnp.mean(yg * yg, axis=-1, keepdims=True) + EPS)
    return (yg.reshape(b, l, D_INNER) * norm_g.astype(jnp.float32)).astype(z.dtype)


def dsa_branch(q, k, v, qi, ki, wi, cos_a, sin_a, cos_i, sin_i, qn_g, kn_g):
    b, l = q.shape[0], q.shape[1]
    q = apply_rope(rms_norm(q.reshape(b, l, ATT_HEADS, ATT_HEAD_DIM), qn_g), cos_a, sin_a)
    k = apply_rope(rms_norm(k.reshape(b, l, ATT_KV_HEADS, ATT_HEAD_DIM), kn_g), cos_a, sin_a)
    v = v.reshape(b, l, ATT_KV_HEADS, ATT_HEAD_DIM)
    qi = apply_rope(qi.reshape(b, l, IDX_HEADS, IDX_DIM), cos_i, sin_i)
    ki = apply_rope(ki.reshape(b, l, 1, IDX_DIM), cos_i, sin_i)[:, :, 0].astype(jnp.float32)
    topk = min(TOPK_MAX, l // 4)
    nblk = l // Q_BLOCK
    rep = ATT_HEADS // ATT_KV_HEADS
    key_pos = jnp.arange(l)

    def blockify(t):
        return jnp.moveaxis(t.reshape(b, nblk, Q_BLOCK, *t.shape[2:]), 1, 0)

    def one_block(args):
        blk, qb, qib, wib = args
        t_pos = blk * Q_BLOCK + jnp.arange(Q_BLOCK)
        s = jnp.einsum('bthd,bsd->bths', qib.astype(jnp.float32), ki) * (IDX_DIM ** -0.5)
        I = jnp.einsum('bth,bths->bts', wib.astype(jnp.float32), jax.nn.relu(s)) * (IDX_HEADS ** -0.5)
        causal = key_pos[None, :] <= t_pos[:, None]
        I = jnp.where(causal[None], I, -jnp.inf)
        _, idx = lax.top_k(I, topk)
        valid = idx <= t_pos[None, :, None]
        k_sel = jax.vmap(lambda kb, ib: kb[ib])(k, idx)
        v_sel = jax.vmap(lambda vb, ib: vb[ib])(v, idx)
        qg = qb.reshape(b, Q_BLOCK, ATT_KV_HEADS, rep, ATT_HEAD_DIM).astype(jnp.float32)
        sc = jnp.einsum('btgrd,btkgd->btgrk', qg, k_sel.astype(jnp.float32)) * (ATT_HEAD_DIM ** -0.5)
        sc = jnp.where(valid[:, :, None, None, :], sc, -jnp.inf)
        pr = jax.nn.softmax(sc, axis=-1)
        o = jnp.einsum('btgrk,btkgd->btgrd', pr, v_sel.astype(jnp.float32))
        return o.reshape(b, Q_BLOCK, ATT_WIDTH).astype(qb.dtype)

    out = lax.map(one_block, (jnp.arange(nblk), blockify(q), blockify(qi), blockify(wi)))
    return jnp.moveaxis(out, 0, 1).reshape(b, l, ATT_WIDTH)


def peer_ffn(xn, w_q, sub_keys, u_tab, v_tab):
    b, l, _ = xn.shape
    q = (xn @ w_q).reshape(b, l, PEER_HEADS, 2, PEER_QDIM // 2).astype(jnp.float32)
    s = jnp.einsum('blhpd,hpkd->blhpk', q, sub_keys.astype(jnp.float32))
    s1, i1 = lax.top_k(s[..., 0, :], PEER_TOPK)
    s2, i2 = lax.top_k(s[..., 1, :], PEER_TOPK)
    cand = (s1[..., :, None] + s2[..., None, :]).reshape(b, l, PEER_HEADS, PEER_TOPK * PEER_TOPK)
    cidx = (i1[..., :, None] * N_KEYS + i2[..., None, :]).reshape(b, l, PEER_HEADS, PEER_TOPK * PEER_TOPK)
    top_s, pos = lax.top_k(cand, PEER_TOPK)
    eidx = jnp.take_along_axis(cidx, pos, axis=-1)
    gate = jax.nn.softmax(top_s, axis=-1)
    nblk = l // PEER_BLOCK

    def blockify(t):
        return jnp.moveaxis(t.reshape(b, nblk, PEER_BLOCK, *t.shape[2:]), 1, 0)

    def one_block(args):
        xb, eb, gb = args
        u = u_tab[eb].astype(jnp.float32)
        a = jax.nn.gelu(jnp.einsum('btd,bthkd->bthk', xb.astype(jnp.float32), u), approximate=False) * gb
        return jnp.einsum('bthk,bthkd->btd', a, v_tab[eb].astype(jnp.float32)).astype(xb.dtype)

    out = lax.map(one_block, (blockify(xn), blockify(eidx), blockify(gate)))
    return jnp.moveaxis(out, 0, 1).reshape(b, l, -1)


def setup_inputs(seed: int = 0) -> dict:
    key = jax.random.key(seed)
    ks = jax.random.split(key, 24)
    f32 = jnp.float32
    n_in = sum(SPLITS)
    col_scale = jnp.concatenate([
        jnp.full((s,), 0.1 if i == 2 else 1.0, f32) for i, s in enumerate(SPLITS)]) * (D_MODEL ** -0.5)
    row_scale = jnp.concatenate([jnp.full((D_INNER,), D_INNER ** -0.5, f32),
                                 jnp.full((ATT_WIDTH,), ATT_WIDTH ** -0.5, f32)])
    dt0 = jnp.exp(jax.random.uniform(ks[6], (DEPTH, SSM_HEADS), f32,
                                     jnp.log(1e-3), jnp.log(1e-1)))
    x = jax.random.normal(ks[0], (BATCH, SEQ, D_MODEL), f32)
    p = jax.random.normal(ks[1], (DEPTH, BATCH, SEQ, PLE_DIM), f32)
    positions = (jax.random.randint(ks[2], (BATCH, 1), 0, 1024)
                 + jnp.arange(SEQ, dtype=jnp.int32)[None, :]).astype(jnp.int32)
    return {
        "x": x,
        "p": p,
        "positions": positions,
        "norm_mix": 1.0 + 0.02 * jax.random.normal(ks[3], (DEPTH, D_MODEL), f32),
        "w_in": jax.random.normal(ks[4], (DEPTH, D_MODEL, n_in), f32) * col_scale,
        "conv_w": 0.5 * jax.random.normal(ks[5], (DEPTH, CONV_DIM, CONV_WIDTH), f32),
        "conv_b": 0.02 * jax.random.normal(ks[7], (DEPTH, CONV_DIM), f32),
        "dt_bias": dt0 + jnp.log(-jnp.expm1(-dt0)),
        "a_log": jnp.log(jax.random.uniform(ks[8], (DEPTH, SSM_HEADS), f32, 1.0, 16.0)),
        "d_skip": 1.0 + 0.02 * jax.random.normal(ks[9], (DEPTH, SSM_HEADS), f32),
        "ssm_norm": 1.0 + 0.02 * jax.random.normal(ks[10], (DEPTH, D_INNER), f32),
        "q_norm": 1.0 + 0.02 * jax.random.normal(ks[11], (DEPTH, ATT_HEAD_DIM), f32),
        "k_norm": 1.0 + 0.02 * jax.random.normal(ks[12], (DEPTH, ATT_HEAD_DIM), f32),
        "w_branch": jax.random.normal(ks[13], (DEPTH, D_INNER + ATT_WIDTH, D_MODEL), f32) * row_scale[:, None],
        "w_out": jax.random.normal(ks[14], (DEPTH, D_MODEL, D_MODEL), f32) * (D_MODEL ** -0.5),
        "norm_ffn": 1.0 + 0.02 * jax.random.normal(ks[15], (DEPTH, D_MODEL), f32),
        "peer_wq": jax.random.normal(ks[16], (DEPTH, D_MODEL, PEER_HEADS * PEER_QDIM), f32) * (D_MODEL ** -0.5),
        "peer_keys": jax.random.normal(ks[17], (DEPTH, PEER_HEADS, 2, N_KEYS, PEER_QDIM // 2), f32) * ((PEER_QDIM // 2) ** -0.5),
        "peer_u": jax.random.normal(ks[18], (DEPTH, N_EXPERTS, D_MODEL), f32) * (D_MODEL ** -0.5),
        "peer_v": jax.random.normal(ks[19], (DEPTH, N_EXPERTS, D_MODEL), f32) * (PEER_HEADS ** -0.5),
        "norm_ple": 1.0 + 0.02 * jax.random.normal(ks[20], (DEPTH, D_MODEL), f32),
        "w_ple_gate": jax.random.normal(ks[21], (DEPTH, D_MODEL, D_MODEL), f32) * (D_MODEL ** -0.5),
        "w_ple_proj": jax.random.normal(ks[22], (DEPTH, PLE_DIM, D_MODEL), f32) * (PLE_DIM ** -0.5),
    }


def reference(x, p, positions, norm_mix, w_in, conv_w, conv_b, dt_bias, a_log, d_skip,
              ssm_norm, q_norm, k_norm, w_branch, w_out, norm_ffn, peer_wq, peer_keys,
              peer_u, peer_v, norm_ple, w_ple_gate, w_ple_proj):
    cos_a, sin_a = rope_tables(positions, ATT_HEAD_DIM)
    cos_i, sin_i = rope_tables(positions, IDX_DIM)
    pts = _split_points()
    for i in range(DEPTH):
        h = rms_norm(x, norm_mix[i])
        proj = h @ w_in[i]
        z, xbc, dt_raw, q, k, v, qi, ki, wi, gates = jnp.split(proj, pts, axis=-1)
        y_ssm = ssm_branch(z, xbc, dt_raw, conv_w[i], conv_b[i], dt_bias[i], a_log[i],
                           d_skip[i], ssm_norm[i])
        y_att = dsa_branch(q, k, v, qi, ki, wi, cos_a, sin_a, cos_i, sin_i,
                           q_norm[i], k_norm[i])
        g_ssm, g_att = jnp.split(jax.nn.sigmoid(gates.astype(jnp.float32)), 2, axis=-1)
        merged = (g_ssm * (y_ssm @ w_branch[i, :D_INNER]).astype(jnp.float32)
                  + g_att * (y_att @ w_branch[i, D_INNER:]).astype(jnp.float32))
        x = x + merged.astype(x.dtype) @ w_out[i]
        h = rms_norm(x, norm_ffn[i])
        x = x + peer_ffn(h, peer_wq[i], peer_keys[i], peer_u[i], peer_v[i])
        h = rms_norm(x, norm_ple[i])
        x = x + (jax.nn.sigmoid(h @ w_ple_gate[i]) * (p[i] @ w_ple_proj[i])).astype(x.dtype)
    return x
```

```python
import functools
import math

import numpy as np
import jax
import jax.numpy as jnp
from jax import lax
from jax.experimental import pallas as pl
from jax.experimental.pallas import tpu as pltpu

F32 = jnp.float32
BF16 = jnp.bfloat16
I32 = jnp.int32

EPS = 1e-6
ROPE_THETA = 10000.0

SSM_HEADS = 32
SSM_HEAD_DIM = 64
D_INNER = SSM_HEADS * SSM_HEAD_DIM
SSM_GROUPS = 4
D_STATE = 128
CONV_WIDTH = 4
CHUNK = 128
CONV_DIM = D_INNER + 2 * SSM_GROUPS * D_STATE
ATT_HEADS = 8
ATT_KV_HEADS = 2
ATT_HEAD_DIM = 128
ATT_WIDTH = ATT_HEADS * ATT_HEAD_DIM
IDX_HEADS = 8
IDX_DIM = 64
TOPK_MAX = 256
Q_BLOCK = 128
PEER_HEADS = 8
N_KEYS = 128
PEER_TOPK = 16
PEER_QDIM = 256

LANES = 128
SUBLANES = 8
VMEM_LIMIT = 56 * 1024 * 1024
INT_MIN = np.int32(-2 ** 31)
NEG_BIG = -1e30

_NT = (((1,), (1,)), ((), ()))


def _params(sem):
    return pltpu.CompilerParams(dimension_semantics=sem, vmem_limit_bytes=VMEM_LIMIT)


def _rms(x, g):
    return x * lax.rsqrt(jnp.mean(x * x, axis=-1, keepdims=True) + EPS) * g


def _sigmoid(x):
    return 1.0 / (1.0 + jnp.exp(-x))


def _split3(a):
    a1 = a.astype(BF16)
    r1 = a - a1.astype(F32)
    a2 = r1.astype(BF16)
    a3 = (r1 - a2.astype(F32)).astype(BF16)
    return a1, a2, a3


def _dot(a, b):
    return jnp.dot(a, b, preferred_element_type=F32)


def _norm_mm_kernel(x_ref, g_ref, w_ref, o_ref):
    h = _rms(x_ref[...], g_ref[...]).astype(BF16)
    o_ref[...] = _dot(h, w_ref[...]).astype(o_ref.dtype)


def norm_matmul(x, g, w, out_dtype=F32, tm=512):
    m, k = x.shape
    n = w.shape[1]
    return pl.pallas_call(
        _norm_mm_kernel,
        grid=(m // tm,),
        in_specs=[pl.BlockSpec((tm, k), lambda i: (i, 0)),
                  pl.BlockSpec((1, k), lambda i: (0, 0)),
                  pl.BlockSpec((k, n), lambda i: (0, 0))],
        out_specs=pl.BlockSpec((tm, n), lambda i: (i, 0)),
        out_shape=jax.ShapeDtypeStruct((m, n), out_dtype),
        compiler_params=_params(("parallel",)),
        name="norm_matmul",
    )(x, g, w)


def _norm_mm_t_kernel(x_ref, g_ref, wt_ref, o_ref):
    h = _rms(x_ref[...], g_ref[...]).astype(BF16)
    o_ref[...] = lax.dot_general(wt_ref[...], h, _NT, preferred_element_type=F32)


def norm_matmul_t(x, g, wt, tm=512):
    m, k = x.shape
    n = wt.shape[0]
    return pl.pallas_call(
        _norm_mm_t_kernel,
        grid=(m // tm,),
        in_specs=[pl.BlockSpec((tm, k), lambda i: (i, 0)),
                  pl.BlockSpec((1, k), lambda i: (0, 0)),
                  pl.BlockSpec((n, k), lambda i: (0, 0))],
        out_specs=pl.BlockSpec((n, tm), lambda i: (0, i)),
        out_shape=jax.ShapeDtypeStruct((n, m), F32),
        compiler_params=_params(("parallel",)),
        name="norm_matmul_t",
    )(x, g, wt)


def _ssm_kernel(z_ref, xbc_ref, dt_ref, convw_ref, convb_ref, dtb_ref, alog_ref,
                dskip_ref, normg_ref, expand_ref, y_ref, carry_ref, state_ref):
    @pl.when(pl.program_id(1) == 0)
    def _():
        carry_ref[...] = jnp.zeros_like(carry_ref)
        state_ref[...] = jnp.zeros_like(state_ref)

    xbc = xbc_ref[...]
    ext = jnp.concatenate([carry_ref[...], xbc], axis=0)
    conv = convb_ref[...]
    for j in range(CONV_WIDTH):
        lo = SUBLANES - (CONV_WIDTH - 1) + j
        conv = conv + convw_ref[j:j + 1, :] * ext[lo:lo + CHUNK]
    carry_ref[...] = xbc[CHUNK - SUBLANES:]
    xc = conv * _sigmoid(conv)
    xs = xc[:, :D_INNER]
    gw = SSM_GROUPS * D_STATE
    bm = xc[:, D_INNER:D_INNER + gw]
    cm = xc[:, D_INNER + gw:]

    lane = lax.broadcasted_iota(I32, (CHUNK, LANES), 1)
    row = lax.broadcasted_iota(I32, (CHUNK, LANES), 0)
    head_lane = lane < SSM_HEADS
    dt = jax.nn.softplus(dt_ref[...] + dtb_ref[...])
    adt = jnp.where(head_lane, dt * (-jnp.exp(alog_ref[...])), 0.0)
    tril = jnp.where(row >= lane, 1.0, 0.0).astype(BF16)
    cs = sum(_dot(tril, part) for part in _split3(adt))
    cs_row = cs.T
    cs_last = cs[CHUNK - 1:CHUNK, :]
    ecs = jnp.exp(cs)
    decay = jnp.exp(cs_last - cs)

    expand = expand_ref[...]

    def widen(a):
        return sum(_dot(part, expand) for part in _split3(a))

    dt_w = widen(dt)
    ecs_w = widen(ecs)
    decay_w = widen(decay)
    xdt = xs * dt_w
    xdt_b = xdt.astype(BF16)
    xdec_b = (xdt * decay_w).astype(BF16)
    causal = row >= lane
    pair_lo = lane < SSM_HEAD_DIM

    gh = SSM_HEADS // SSM_GROUPS
    gcols = gh * SSM_HEAD_DIM
    ys = []
    for g in range(SSM_GROUPS):
        bg = bm[:, g * D_STATE:(g + 1) * D_STATE]
        cg = cm[:, g * D_STATE:(g + 1) * D_STATE].astype(BF16)
        cb = lax.dot_general(cg, bg.astype(BF16), _NT, preferred_element_type=F32)
        st = state_ref[g]
        y_off = _dot(cg, st.astype(BF16))
        new_st = _dot(bg.T.astype(BF16), xdec_b[:, g * gcols:(g + 1) * gcols])
        state_ref[g] = st * ecs_w[CHUNK - 1:CHUNK, g * gcols:(g + 1) * gcols] + new_st
        pieces = []
        for pr in range(gh // 2):
            outs = []
            for sub in range(2):
                h = g * gh + pr * 2 + sub
                lm = jnp.where(causal, jnp.exp(cs[:, h:h + 1] - cs_row[h:h + 1, :]), 0.0)
                mm = (cb * lm).astype(BF16)
                c0 = (h // 2) * LANES
                outs.append(_dot(mm, xdt_b[:, c0:c0 + LANES]))
            pieces.append(jnp.where(pair_lo, outs[0], outs[1]))
        y_diag = jnp.concatenate(pieces, axis=1)
        ys.append(y_diag + y_off * ecs_w[:, g * gcols:(g + 1) * gcols])
    y = jnp.concatenate(ys, axis=1) + dskip_ref[...] * xs
    z = z_ref[...]
    y = y * (z * _sigmoid(z))
    outs = []
    for g in range(SSM_GROUPS):
        yg = y[:, g * gcols:(g + 1) * gcols]
        outs.append(yg * lax.rsqrt(jnp.mean(yg * yg, axis=-1, keepdims=True) + EPS))
    y_ref[...] = (jnp.concatenate(outs, axis=1) * normg_ref[...]).astype(y_ref.dtype)


def ssm_branch(z, xbc, dt_slab, conv_w, conv_b, dt_bias, a_log, d_skip, norm_g, batch, seq):
    t = batch * seq
    nchunk = seq // CHUNK
    pad = LANES - SSM_HEADS
    convw_t = jnp.transpose(conv_w)
    dtb = jnp.pad(dt_bias, (0, pad)).reshape(1, LANES)
    alog = jnp.pad(a_log, (0, pad)).reshape(1, LANES)
    dskip = jnp.repeat(d_skip, SSM_HEAD_DIM).reshape(1, D_INNER)
    expand = (np.arange(LANES)[:, None] == (np.arange(D_INNER)[None, :] // SSM_HEAD_DIM))
    expand = jnp.asarray(expand, dtype=BF16)
    tok = lambda w: pl.BlockSpec((CHUNK, w), lambda b, c: (b * nchunk + c, 0))
    const = lambda r, w: pl.BlockSpec((r, w), lambda b, c: (0, 0))
    return pl.pallas_call(
        _ssm_kernel,
        grid=(batch, nchunk),
        in_specs=[tok(D_INNER), tok(CONV_DIM), tok(LANES),
                  const(CONV_WIDTH, CONV_DIM), const(1, CONV_DIM), const(1, LANES),
                  const(1, LANES), const(1, D_INNER), const(1, D_INNER),
                  const(LANES, D_INNER)],
        out_specs=tok(D_INNER),
        out_shape=jax.ShapeDtypeStruct((t, D_INNER), BF16),
        scratch_shapes=[pltpu.VMEM((SUBLANES, CONV_DIM), F32),
                        pltpu.VMEM((SSM_GROUPS, D_STATE, D_INNER // SSM_GROUPS), F32)],
        compiler_params=_params(("parallel", "arbitrary")),
        name="ssm_branch",
    )(z, xbc, dt_slab, convw_t, conv_b.reshape(1, CONV_DIM), dtb, alog, dskip,
      norm_g.reshape(1, D_INNER), expand)


def _rope_kernel(pos_ref, inva_ref, invi_ref, ca_ref, sa_ref, ci_ref, si_ref):
    pos = pos_ref[...]
    ang_a = inva_ref[...] * pos
    ang_i = invi_ref[...] * pos
    ca_ref[...] = jnp.cos(ang_a)
    sa_ref[...] = jnp.sin(ang_a)
    ci_ref[...] = jnp.cos(ang_i)
    si_ref[...] = jnp.sin(ang_i)


def rope_tables_t(positions, tile=2048):
    t = positions.size
    tile = min(tile, t)
    pos = positions.reshape(1, t).astype(F32)
    ha, hi = ATT_HEAD_DIM // 2, IDX_DIM // 2
    inv_a = (1.0 / (ROPE_THETA ** (jnp.arange(0, ATT_HEAD_DIM, 2, dtype=F32) / ATT_HEAD_DIM))).reshape(ha, 1)
    inv_i = (1.0 / (ROPE_THETA ** (jnp.arange(0, IDX_DIM, 2, dtype=F32) / IDX_DIM))).reshape(hi, 1)
    col = lambda r: pl.BlockSpec((r, tile), lambda i: (0, i))
    return pl.pallas_call(
        _rope_kernel,
        grid=(t // tile,),
        in_specs=[col(1), pl.BlockSpec((ha, 1), lambda i: (0, 0)), pl.BlockSpec((hi, 1), lambda i: (0, 0))],
        out_specs=[col(ha), col(ha), col(hi), col(hi)],
        out_shape=[jax.ShapeDtypeStruct((ha, t), F32)] * 2 + [jax.ShapeDtypeStruct((hi, t), F32)] * 2,
        compiler_params=_params(("parallel",)),
        name="rope_tables",
    )(pos, inv_a, inv_i)


_Q0 = 0
_K0 = _Q0 + ATT_WIDTH
_V0 = _K0 + ATT_KV_HEADS * ATT_HEAD_DIM
_QI0 = _V0 + ATT_KV_HEADS * ATT_HEAD_DIM
_KI0 = _QI0 + IDX_HEADS * IDX_DIM
_WI0 = _KI0 + IDX_DIM
_ATT_ROWS = _WI0 + IDX_HEADS


def _rope_rows(x, cos, sin):
    half = x.shape[0] // 2
    x1, x2 = x[:half], x[half:]
    return jnp.concatenate([x1 * cos - x2 * sin, x1 * sin + x2 * cos], axis=0)


def _head_norm_rows(x, g):
    return x * lax.rsqrt(jnp.mean(x * x, axis=0, keepdims=True) + EPS) * g


def _dsa_prep_kernel(p_ref, ca_ref, sa_ref, ci_ref, si_ref, qg_ref, kg_ref,
                     qt_ref, k_ref, v_ref, qit_ref, ki_ref, wit_ref):
    tile = p_ref.shape[1]
    ca, sa, ci, si = ca_ref[...], sa_ref[...], ci_ref[...], si_ref[...]
    hd = ATT_HEAD_DIM
    for h in range(ATT_HEADS):
        x = _head_norm_rows(p_ref[_Q0 + h * hd:_Q0 + (h + 1) * hd, :], qg_ref[...])
        qt_ref[h * hd:(h + 1) * hd, :] = (_rope_rows(x, ca, sa) * (hd ** -0.5)).astype(BF16)
    for h in range(ATT_KV_HEADS):
        x = _head_norm_rows(p_ref[_K0 + h * hd:_K0 + (h + 1) * hd, :], kg_ref[...])
        k_ref[:, h * hd:(h + 1) * hd] = _rope_rows(x, ca, sa).T.astype(BF16)
    for s in range(tile // LANES):
        v_ref[s] = p_ref[_V0:_V0 + ATT_KV_HEADS * hd, s * LANES:(s + 1) * LANES].astype(BF16)
    zpad = jnp.zeros((LANES - IDX_DIM, tile), F32)
    for h in range(IDX_HEADS):
        x = _rope_rows(p_ref[_QI0 + h * IDX_DIM:_QI0 + (h + 1) * IDX_DIM, :], ci, si) * (IDX_DIM ** -0.5)
        qit_ref[h * LANES:(h + 1) * LANES, :] = jnp.concatenate([x, zpad], axis=0).astype(BF16)
    x = _rope_rows(p_ref[_KI0:_KI0 + IDX_DIM, :], ci, si)
    ki_ref[...] = jnp.concatenate([x, zpad], axis=0).T.astype(BF16)
    wit_ref[...] = p_ref[_WI0:_WI0 + IDX_HEADS, :] * (IDX_HEADS ** -0.5)


def dsa_prep(proj_t, tables, q_norm, k_norm, tile=512):
    t = proj_t.shape[1]
    tile = min(tile, t)
    ca, sa, ci, si = tables
    ha, hi = ATT_HEAD_DIM // 2, IDX_DIM // 2
    col = lambda r: pl.BlockSpec((r, tile), lambda i: (0, i))
    rowb = lambda w: pl.BlockSpec((tile, w), lambda i: (i, 0))
    const = lambda r, w: pl.BlockSpec((r, w), lambda i: (0, 0))
    kvw = ATT_KV_HEADS * ATT_HEAD_DIM
    return pl.pallas_call(
        _dsa_prep_kernel,
        grid=(t // tile,),
        in_specs=[col(_ATT_ROWS), col(ha), col(ha), col(hi), col(hi),
                  const(ATT_HEAD_DIM, 1), const(ATT_HEAD_DIM, 1)],
        out_specs=[col(ATT_WIDTH), rowb(kvw),
                   pl.BlockSpec((tile // LANES, kvw, LANES), lambda i: (i, 0, 0)),
                   col(IDX_HEADS * LANES), rowb(LANES), col(IDX_HEADS)],
        out_shape=[jax.ShapeDtypeStruct((ATT_WIDTH, t), BF16),
                   jax.ShapeDtypeStruct((t, kvw), BF16),
                   jax.ShapeDtypeStruct((t // LANES, kvw, LANES), BF16),
                   jax.ShapeDtypeStruct((IDX_HEADS * LANES, t), BF16),
                   jax.ShapeDtypeStruct((t, LANES), BF16),
                   jax.ShapeDtypeStruct((IDX_HEADS, t), F32)],
        compiler_params=_params(("parallel",)),
        name="dsa_prep",
    )(proj_t, ca, sa, ci, si, q_norm.reshape(ATT_HEAD_DIM, 1), k_norm.reshape(ATT_HEAD_DIM, 1))


def _dsa_kernel(ki_ref, k_ref, v_ref, qit_ref, qt_ref, wit_ref, o_ref, keys_ref, bias_ref,
                *, topk, idx_bits):
    qb = Q_BLOCK
    j = pl.program_id(1)
    nk = j + 1
    row = lax.broadcasted_iota(I32, (qb, qb), 0)
    col = lax.broadcasted_iota(I32, (qb, qb), 1)
    q_pos = j * qb + col

    def score_chunk(c, carry):
        ks = pl.multiple_of(c * qb, qb)
        kic = ki_ref[pl.ds(ks, qb), :]
        acc = jnp.zeros((qb, qb), F32)
        for h in range(IDX_HEADS):
            s = _dot(kic, qit_ref[h * LANES:(h + 1) * LANES, :])
            acc = acc + wit_ref[h:h + 1, :] * jnp.maximum(s, 0.0)
        acc = jnp.where(acc == 0.0, 0.0, acc)
        bits = pltpu.bitcast(acc, I32)
        key = jnp.where(bits < 0, bits ^ np.int32(0x7FFFFFFF), bits)
        keys_ref[pl.ds(ks, qb), :] = jnp.where(ks + row <= q_pos, key, INT_MIN)
        return carry

    lax.fori_loop(0, nk, score_chunk, 0)

    def count(hit):
        def body(c, acc):
            ks = pl.multiple_of(c * qb, qb)
            one = hit(keys_ref[pl.ds(ks, qb), :], ks + row)
            return acc + one.reshape(qb // SUBLANES, SUBLANES, qb).sum(axis=0)
        acc = lax.fori_loop(0, nk, body, jnp.zeros((SUBLANES, qb), I32))
        return acc.sum(axis=0, keepdims=True)

    def value_bit(i, prefix):
        cand = prefix | jnp.left_shift(np.int32(1), 31 - i)
        cand_s = cand ^ INT_MIN
        n = count(lambda key, idx: jnp.where(key >= cand_s, 1, 0))
        return jnp.where(n >= topk, cand, prefix)

    prefix = lax.fori_loop(0, 32, value_bit, jnp.zeros((1, qb), I32))
    thr = prefix ^ INT_MIN

    n_gt = count(lambda key, idx: jnp.where(key > thr, 1, 0))
    n_ge = count(lambda key, idx: jnp.where(key >= thr, 1, 0))
    need = topk - n_gt
    excess = jnp.max(jnp.where(thr > INT_MIN, n_ge - n_gt - need, 0))

    def tie_search():
        def index_bit(i, prefix):
            cand = prefix | jnp.left_shift(np.int32(1), idx_bits - 1 - i)
            n = count(lambda key, idx: jnp.where(key == thr, jnp.where(idx < cand, 1, 0), 0))
            return jnp.where(n < need, cand, prefix)
        return lax.fori_loop(0, idx_bits, index_bit, jnp.zeros((1, qb), I32))

    last = lax.cond(excess > 0, tie_search, lambda: jnp.full((1, qb), 2 ** idx_bits, I32))

    def bias_chunk(c, carry):
        ks = pl.multiple_of(c * qb, qb)
        key = keys_ref[pl.ds(ks, qb), :]
        idx = ks + row
        tied = jnp.where(key == thr, jnp.where(idx <= last, 0.0, NEG_BIG), NEG_BIG)
        bias = jnp.where(key > thr, 0.0, tied)
        bias_ref[pl.ds(ks, qb), :] = jnp.where(idx <= q_pos, bias, NEG_BIG)
        return carry

    lax.fori_loop(0, nk, bias_chunk, 0)

    rep = ATT_HEADS // ATT_KV_HEADS
    hd = ATT_HEAD_DIM
    for g in range(ATT_KV_HEADS):
        qg = jnp.concatenate([qt_ref[(g * rep + r) * hd:(g * rep + r + 1) * hd, :] for r in range(rep)], axis=1)

        def att_chunk(c, carry):
            m, l, acc = carry
            ks = pl.multiple_of(c * qb, qb)
            s = _dot(k_ref[pl.ds(ks, qb), g * hd:(g + 1) * hd], qg)
            b = bias_ref[pl.ds(ks, qb), :]
            s = s + jnp.concatenate([b] * rep, axis=1)
            m_new = jnp.maximum(m, jnp.max(s, axis=0, keepdims=True))
            alpha = jnp.exp(m - m_new)
            p = jnp.exp(s - m_new)
            l = alpha * l + jnp.sum(p, axis=0, keepdims=True)
            acc = alpha * acc + _dot(v_ref[c, g * hd:(g + 1) * hd, :], p.astype(BF16))
            return m_new, l, acc

        init = (jnp.full((1, rep * qb), NEG_BIG, F32), jnp.zeros((1, rep * qb), F32),
                jnp.zeros((hd, rep * qb), F32))
        _, l, acc = lax.fori_loop(0, nk, att_chunk, init)
        o = acc / l
        for r in range(rep):
            h = g * rep + r
            o_ref[:, h * hd:(h + 1) * hd] = o[:, r * qb:(r + 1) * qb].T.astype(o_ref.dtype)


def dsa_attention(ki, k, v3, qit, qt, wit, batch, seq):
    t = batch * seq
    nqb = seq // Q_BLOCK
    kvw = ATT_KV_HEADS * ATT_HEAD_DIM
    topk = min(TOPK_MAX, seq // 4)
    idx_bits = max(1, int(math.ceil(math.log2(seq))))
    qcol = lambda r: pl.BlockSpec((r, Q_BLOCK), lambda b, j: (0, b * nqb + j))
    return pl.pallas_call(
        functools.partial(_dsa_kernel, topk=topk, idx_bits=idx_bits),
        grid=(batch, nqb),
        in_specs=[pl.BlockSpec((seq, LANES), lambda b, j: (b, 0)),
                  pl.BlockSpec((seq, kvw), lambda b, j: (b, 0)),
                  pl.BlockSpec((seq // LANES, kvw, LANES), lambda b, j: (b, 0, 0)),
                  qcol(IDX_HEADS * LANES), qcol(ATT_WIDTH), qcol(IDX_HEADS)],
        out_specs=pl.BlockSpec((Q_BLOCK, ATT_WIDTH), lambda b, j: (b * nqb + j, 0)),
        out_shape=jax.ShapeDtypeStruct((t, ATT_WIDTH), BF16),
        scratch_shapes=[pltpu.VMEM((seq, Q_BLOCK), I32), pltpu.VMEM((seq, Q_BLOCK), F32)],
        compiler_params=_params(("parallel", "arbitrary")),
        name="dsa_attention",
    )(ki, k, v3, qit, qt, wit)


def _merge_kernel(x_ref, ys_ref, ya_ref, gate_ref, wbs_ref, wba_ref, wo_ref, o_ref):
    d = x_ref.shape[1]
    gates = gate_ref[...]
    merged = (_sigmoid(gates[:, :d]) * _dot(ys_ref[...], wbs_ref[...])
              + _sigmoid(gates[:, d:]) * _dot(ya_ref[...], wba_ref[...]))
    o_ref[...] = x_ref[...] + _dot(merged.astype(BF16), wo_ref[...])


def merge_branches(x, y_ssm, y_att, gates, w_bs, w_ba, w_o, tm=512):
    t, d = x.shape
    tm = min(tm, t)
    tok = lambda w: pl.BlockSpec((tm, w), lambda i: (i, 0))
    const = lambda r, w: pl.BlockSpec((r, w), lambda i: (0, 0))
    return pl.pallas_call(
        _merge_kernel,
        grid=(t // tm,),
        in_specs=[tok(d), tok(D_INNER), tok(ATT_WIDTH), tok(2 * d),
                  const(D_INNER, d), const(ATT_WIDTH, d), const(d, d)],
        out_specs=tok(d),
        out_shape=jax.ShapeDtypeStruct((t, d), F32),
        compiler_params=_params(("parallel",)),
        name="merge_branches",
    )(x, y_ssm, y_att, gates, w_bs, w_ba, w_o)


def _top_rows(s, k):
    rows = lax.broadcasted_iota(I32, (k, s.shape[1]), 0)

    def body(r, carry):
        s, top = carry
        m = jnp.max(s, axis=0, keepdims=True)
        return jnp.where(s == m, -jnp.inf, s), jnp.where(rows == r, m, top)

    _, top = lax.fori_loop(0, k, body, (s, jnp.zeros((k, s.shape[1]), F32)))
    return top


def _peer_route_kernel(x_ref, g_ref, wqt_ref, keys_ref, ht_ref, s1_ref, s2_ref, e1_ref, e2_ref, c_ref):
    tm = x_ref.shape[0]
    h = _rms(x_ref[...], g_ref[...])
    ht_ref[...] = h.T.astype(BF16)
    qt = lax.dot_general(wqt_ref[...], h.astype(BF16), _NT, preferred_element_type=F32).astype(BF16)
    half = PEER_QDIM // 2
    k = PEER_TOPK
    sub8 = lax.broadcasted_iota(I32, (SUBLANES, tm), 0)
    for hh in range(PEER_HEADS):
        s1 = _dot(keys_ref[2 * hh], qt[(2 * hh) * half:(2 * hh + 1) * half, :])
        s2 = _dot(keys_ref[2 * hh + 1], qt[(2 * hh + 1) * half:(2 * hh + 2) * half, :])
        a = _top_rows(s1, k)
        b = _top_rows(s2, k)
        tiles = [a[:SUBLANES] + b[0:1], a[SUBLANES:] + b[0:1]]
        for jj in range(1, k):
            tiles.append(jnp.where(sub8 < k // (jj + 1), a[:SUBLANES] + b[jj:jj + 1], -jnp.inf))
        cand = jnp.concatenate(tiles, axis=0)
        best = a[0:1] + b[0:1]

        def body(r, carry):
            cand, zsum, kth = carry
            m = jnp.max(cand, axis=0, keepdims=True)
            return jnp.where(cand == m, -jnp.inf, cand), zsum + jnp.exp(m - best), m

        _, zsum, kth = lax.fori_loop(0, k, body, (cand, jnp.zeros((1, tm), F32), best))
        r0 = hh * N_KEYS
        s1_ref[r0:r0 + N_KEYS, :] = s1
        s2_ref[r0:r0 + N_KEYS, :] = s2
        e1_ref[r0:r0 + N_KEYS, :] = jnp.exp(s1 - a[0:1]) / zsum
        e2_ref[r0:r0 + N_KEYS, :] = jnp.exp(s2 - b[0:1])
        c_ref[hh:hh + 1, :] = kth


def peer_route(x, g, wq_t, keys, tm=256):
    t, d = x.shape
    tm = min(tm, t)
    rows = PEER_HEADS * N_KEYS
    col = lambda r: pl.BlockSpec((r, tm), lambda i: (0, i))
    return pl.pallas_call(
        _peer_route_kernel,
        grid=(t // tm,),
        in_specs=[pl.BlockSpec((tm, d), lambda i: (i, 0)),
                  pl.BlockSpec((1, d), lambda i: (0, 0)),
                  pl.BlockSpec(wq_t.shape, lambda i: (0, 0)),
                  pl.BlockSpec(keys.shape, lambda i: (0, 0, 0))],
        out_specs=[col(d), col(rows), col(rows), col(rows), col(rows), col(PEER_HEADS)],
        out_shape=[jax.ShapeDtypeStruct((d, t), BF16)] + [jax.ShapeDtypeStruct((rows, t), F32)] * 4
                  + [jax.ShapeDtypeStruct((PEER_HEADS, t), F32)],
        compiler_params=_params(("parallel",)),
        name="peer_route",
    )(x, g, wq_t, keys)


def _peer_dense_kernel(x_ref, ht_ref, s1_ref, s2_ref, e1_ref, e2_ref, c_ref, u_ref, vt_ref, o_ref, acc_ref,
                       *, ib):
    j = pl.program_id(1)

    @pl.when(j == 0)
    def _():
        acc_ref[...] = jnp.zeros_like(acc_ref)

    hid = _dot(u_ref[...], ht_ref[...])
    parts = []
    for ii in range(ib):
        i1 = j * ib + ii
        gate = jnp.zeros((N_KEYS, hid.shape[1]), F32)
        for hh in range(PEER_HEADS):
            r0 = hh * N_KEYS
            pair = s1_ref[pl.ds(r0 + i1, 1), :] + s2_ref[r0:r0 + N_KEYS, :]
            w = jnp.where(pair >= c_ref[hh:hh + 1, :], e2_ref[r0:r0 + N_KEYS, :], 0.0)
            gate = gate + w * e1_ref[pl.ds(r0 + i1, 1), :]
        hb = hid[ii * N_KEYS:(ii + 1) * N_KEYS]
        act = 0.5 * hb * (1.0 + lax.erf(hb * np.float32(math.sqrt(0.5))))
        parts.append((act * gate).astype(BF16))
    acc_ref[...] += _dot(vt_ref[...], jnp.concatenate(parts, axis=0))

    @pl.when(j == pl.num_programs(1) - 1)
    def _():
        o_ref[...] = x_ref[...] + acc_ref[...].T


def peer_dense(x, h_t, s1, s2, e1, e2, kth, u, v_t, tt=256, ib=4):
    t, d = x.shape
    tt = min(tt, t)
    rows = PEER_HEADS * N_KEYS
    col = lambda r: pl.BlockSpec((r, tt), lambda i, j: (0, i))
    tok = pl.BlockSpec((tt, d), lambda i, j: (i, 0))
    return pl.pallas_call(
        functools.partial(_peer_dense_kernel, ib=ib),
        grid=(t // tt, N_KEYS // ib),
        in_specs=[tok, col(d), col(rows), col(rows), col(rows), col(rows), col(PEER_HEADS),
                  pl.BlockSpec((ib * N_KEYS, d), lambda i, j: (j, 0)),
                  pl.BlockSpec((d, ib * N_KEYS), lambda i, j: (0, j))],
        out_specs=tok,
        out_shape=jax.ShapeDtypeStruct((t, d), F32),
        scratch_shapes=[pltpu.VMEM((d, tt), F32)],
        compiler_params=_params(("parallel", "arbitrary")),
        name="peer_dense",
    )(x, h_t, s1, s2, e1, e2, kth, u, v_t)


def _ple_kernel(x_ref, p_ref, g_ref, wg_ref, wp_ref, o_ref):
    x = x_ref[...]
    h = _rms(x, g_ref[...]).astype(BF16)
    gate = _sigmoid(_dot(h, wg_ref[...]))
    o_ref[...] = x + gate * _dot(p_ref[...].astype(BF16), wp_ref[...])


def ple_block(x, p, g, w_gate, w_proj, tm=512):
    t, d = x.shape
    tm = min(tm, t)
    pd = p.shape[1]
    return pl.pallas_call(
        _ple_kernel,
        grid=(t // tm,),
        in_specs=[pl.BlockSpec((tm, d), lambda i: (i, 0)),
                  pl.BlockSpec((tm, pd), lambda i: (i, 0)),
                  pl.BlockSpec((1, d), lambda i: (0, 0)),
                  pl.BlockSpec((d, d), lambda i: (0, 0)),
                  pl.BlockSpec((pd, d), lambda i: (0, 0))],
        out_specs=pl.BlockSpec((tm, d), lambda i: (i, 0)),
        out_shape=jax.ShapeDtypeStruct((t, d), F32),
        compiler_params=_params(("parallel",)),
        name="ple_block",
    )(x, p, g, w_gate, w_proj)


def _split_w_in(w):
    sizes = [D_INNER, CONV_DIM, SSM_HEADS, ATT_WIDTH, ATT_KV_HEADS * ATT_HEAD_DIM,
             ATT_KV_HEADS * ATT_HEAD_DIM, IDX_HEADS * IDX_DIM, IDX_DIM, IDX_HEADS, 2 * w.shape[0]]
    out, acc = [], 0
    for s in sizes:
        out.append(w[:, acc:acc + s])
        acc += s
    return out


def kernel(x, p, positions, norm_mix, w_in, conv_w, conv_b, dt_bias, a_log, d_skip, ssm_norm, q_norm, k_norm,
           w_branch, w_out, norm_ffn, peer_wq, peer_keys, peer_u, peer_v, norm_ple, w_ple_gate, w_ple_proj):
    batch, seq, d = x.shape
    t = batch * seq
    depth = w_in.shape[0]
    tables = rope_tables_t(positions)
    xf = x.reshape(t, d)
    for i in range(depth):
        wz, wxbc, wdt, wq, wk, wv, wqi, wki, wwi, wg = _split_w_in(w_in[i])
        w_dt = jnp.pad(wdt, ((0, 0), (0, LANES - SSM_HEADS))).astype(BF16)
        w_att_t = jnp.transpose(jnp.concatenate([wq, wk, wv, wqi, wki, wwi], axis=1)).astype(BF16)
        g_mix = norm_mix[i].reshape(1, d)

        z = norm_matmul(xf, g_mix, wz.astype(BF16))
        xbc = norm_matmul(xf, g_mix, wxbc.astype(BF16))
        dt_slab = norm_matmul(xf, g_mix, w_dt)
        gates = norm_matmul(xf, g_mix, wg.astype(BF16))
        proj_t = norm_matmul_t(xf, g_mix, w_att_t)

        y_ssm = ssm_branch(z, xbc, dt_slab, conv_w[i], conv_b[i], dt_bias[i], a_log[i], d_skip[i],
                           ssm_norm[i], batch, seq)
        qt, k, v3, qit, ki, wit = dsa_prep(proj_t, tables, q_norm[i], k_norm[i])
        y_att = dsa_attention(ki, k, v3, qit, qt, wit, batch, seq)
        xf = merge_branches(xf, y_ssm, y_att, gates, w_branch[i, :D_INNER].astype(BF16),
                            w_branch[i, D_INNER:].astype(BF16), w_out[i].astype(BF16))

        keys = peer_keys[i].reshape(PEER_HEADS * 2, N_KEYS, PEER_QDIM // 2).astype(BF16)
        h_t, s1, s2, e1, e2, kth = peer_route(xf, norm_ffn[i].reshape(1, d),
                                              jnp.transpose(peer_wq[i]).astype(BF16), keys)
        xf = peer_dense(xf, h_t, s1, s2, e1, e2, kth, peer_u[i].astype(BF16),
                        jnp.transpose(peer_v[i]).astype(BF16))

        xf = ple_block(xf, p[i].reshape(t, -1), norm_ple[i].reshape(1, d),
                       w_ple_gate[i].astype(BF16), w_ple_proj[i].astype(BF16))
    return xf.reshape(batch, seq, d)
```

```python
import functools
import math

import numpy as np
import jax
import jax.numpy as jnp
from jax import lax
from jax.experimental import pallas as pl
from jax.experimental.pallas import tpu as pltpu

F32 = jnp.float32
BF16 = jnp.bfloat16
I32 = jnp.int32

EPS = 1e-6
ROPE_THETA = 10000.0

SSM_HEADS = 32
SSM_HEAD_DIM = 64
D_INNER = SSM_HEADS * SSM_HEAD_DIM
SSM_GROUPS = 4
D_STATE = 128
CONV_WIDTH = 4
CHUNK = 128
CONV_DIM = D_INNER + 2 * SSM_GROUPS * D_STATE
ATT_HEADS = 8
ATT_KV_HEADS = 2
ATT_HEAD_DIM = 128
ATT_WIDTH = ATT_HEADS * ATT_HEAD_DIM
IDX_HEADS = 8
IDX_DIM = 64
TOPK_MAX = 256
Q_BLOCK = 128
KEY_CHUNK = 256
PEER_HEADS = 8
N_KEYS = 128
PEER_TOPK = 16
PEER_QDIM = 256

LANES = 128
SUBLANES = 8
VMEM_LIMIT = 56 * 1024 * 1024
INT_MIN = np.int32(-2 ** 31)
NEG_BIG = -1e30

_NT = (((1,), (1,)), ((), ()))


def _params(sem):
    return pltpu.CompilerParams(dimension_semantics=sem, vmem_limit_bytes=VMEM_LIMIT)


def _rms(x, g):
    return x * lax.rsqrt(jnp.mean(x * x, axis=-1, keepdims=True) + EPS) * g


def _sigmoid(x):
    return 1.0 / (1.0 + jnp.exp(-x))


def _split3(a):
    a1 = a.astype(BF16)
    r1 = a - a1.astype(F32)
    a2 = r1.astype(BF16)
    a3 = (r1 - a2.astype(F32)).astype(BF16)
    return a1, a2, a3


def _dot(a, b):
    return jnp.dot(a, b, preferred_element_type=F32)


def _norm_mm_kernel(x_ref, g_ref, w_ref, o_ref):
    h = _rms(x_ref[...], g_ref[...]).astype(BF16)
    o_ref[...] = _dot(h, w_ref[...]).astype(o_ref.dtype)


def norm_matmul(x, g, w, out_dtype=F32, tm=512):
    m, k = x.shape
    n = w.shape[1]
    return pl.pallas_call(
        _norm_mm_kernel,
        grid=(m // tm,),
        in_specs=[pl.BlockSpec((tm, k), lambda i: (i, 0)),
                  pl.BlockSpec((1, k), lambda i: (0, 0)),
                  pl.BlockSpec((k, n), lambda i: (0, 0))],
        out_specs=pl.BlockSpec((tm, n), lambda i: (i, 0)),
        out_shape=jax.ShapeDtypeStruct((m, n), out_dtype),
        compiler_params=_params(("parallel",)),
        name="norm_matmul",
    )(x, g, w)


def _norm_mm_t_kernel(x_ref, g_ref, wt_ref, o_ref):
    h = _rms(x_ref[...], g_ref[...]).astype(BF16)
    o_ref[...] = lax.dot_general(wt_ref[...], h, _NT, preferred_element_type=F32)


def norm_matmul_t(x, g, wt, tm=512):
    m, k = x.shape
    n = wt.shape[0]
    return pl.pallas_call(
        _norm_mm_t_kernel,
        grid=(m // tm,),
        in_specs=[pl.BlockSpec((tm, k), lambda i: (i, 0)),
                  pl.BlockSpec((1, k), lambda i: (0, 0)),
                  pl.BlockSpec((n, k), lambda i: (0, 0))],
        out_specs=pl.BlockSpec((n, tm), lambda i: (0, i)),
        out_shape=jax.ShapeDtypeStruct((n, m), F32),
        compiler_params=_params(("parallel",)),
        name="norm_matmul_t",
    )(x, g, wt)


def _ssm_kernel(z_ref, xbc_ref, dt_ref, convw_ref, convb_ref, dtb_ref, alog_ref,
                dskip_ref, normg_ref, expand_ref, y_ref, carry_ref, state_ref):
    @pl.when(pl.program_id(1) == 0)
    def _():
        carry_ref[...] = jnp.zeros_like(carry_ref)
        state_ref[...] = jnp.zeros_like(state_ref)

    xbc = xbc_ref[...]
    ext = jnp.concatenate([carry_ref[...], xbc], axis=0)
    conv = convb_ref[...]
    for j in range(CONV_WIDTH):
        lo = SUBLANES - (CONV_WIDTH - 1) + j
        conv = conv + convw_ref[j:j + 1, :] * ext[lo:lo + CHUNK]
    carry_ref[...] = xbc[CHUNK - SUBLANES:]
    xc = conv * _sigmoid(conv)
    xs = xc[:, :D_INNER]
    gw = SSM_GROUPS * D_STATE
    bm = xc[:, D_INNER:D_INNER + gw]
    cm = xc[:, D_INNER + gw:]

    lane = lax.broadcasted_iota(I32, (CHUNK, LANES), 1)
    row = lax.broadcasted_iota(I32, (CHUNK, LANES), 0)
    head_lane = lane < SSM_HEADS
    dt = jax.nn.softplus(dt_ref[...] + dtb_ref[...])
    adt = jnp.where(head_lane, dt * (-jnp.exp(alog_ref[...])), 0.0)
    tril = jnp.where(row >= lane, 1.0, 0.0).astype(BF16)
    cs = sum(_dot(tril, part) for part in _split3(adt))
    cs_row = cs.T
    cs_last = cs[CHUNK - 1:CHUNK, :]
    ecs = jnp.exp(cs)
    decay = jnp.exp(cs_last - cs)

    expand = expand_ref[...]

    def widen(a):
        return sum(_dot(part, expand) for part in _split3(a))

    dt_w = widen(dt)
    ecs_w = widen(ecs)
    decay_w = widen(decay)
    xdt = xs * dt_w
    xdt_b = xdt.astype(BF16)
    xdec_b = (xdt * decay_w).astype(BF16)
    causal = row >= lane
    pair_lo = lane < SSM_HEAD_DIM

    gh = SSM_HEADS // SSM_GROUPS
    gcols = gh * SSM_HEAD_DIM
    ys = []
    for g in range(SSM_GROUPS):
        bg = bm[:, g * D_STATE:(g + 1) * D_STATE]
        cg = cm[:, g * D_STATE:(g + 1) * D_STATE].astype(BF16)
        cb = lax.dot_general(cg, bg.astype(BF16), _NT, preferred_element_type=F32)
        st = state_ref[g]
        y_off = _dot(cg, st.astype(BF16))
        new_st = _dot(bg.T.astype(BF16), xdec_b[:, g * gcols:(g + 1) * gcols])
        state_ref[g] = st * ecs_w[CHUNK - 1:CHUNK, g * gcols:(g + 1) * gcols] + new_st
        pieces = []
        for pr in range(gh // 2):
            outs = []
            for sub in range(2):
                h = g * gh + pr * 2 + sub
                lm = jnp.where(causal, jnp.exp(cs[:, h:h + 1] - cs_row[h:h + 1, :]), 0.0)
                mm = (cb * lm).astype(BF16)
                c0 = (h // 2) * LANES
                outs.append(_dot(mm, xdt_b[:, c0:c0 + LANES]))
            pieces.append(jnp.where(pair_lo, outs[0], outs[1]))
        y_diag = jnp.concatenate(pieces, axis=1)
        ys.append(y_diag + y_off * ecs_w[:, g * gcols:(g + 1) * gcols])
    y = jnp.concatenate(ys, axis=1) + dskip_ref[...] * xs
    z = z_ref[...]
    y = y * (z * _sigmoid(z))
    outs = []
    for g in range(SSM_GROUPS):
        yg = y[:, g * gcols:(g + 1) * gcols]
        outs.append(yg * lax.rsqrt(jnp.mean(yg * yg, axis=-1, keepdims=True) + EPS))
    y_ref[...] = (jnp.concatenate(outs, axis=1) * normg_ref[...]).astype(y_ref.dtype)


def ssm_branch(z, xbc, dt_slab, conv_w, conv_b, dt_bias, a_log, d_skip, norm_g, batch, seq):
    t = batch * seq
    nchunk = seq // CHUNK
    pad = LANES - SSM_HEADS
    convw_t = jnp.transpose(conv_w)
    dtb = jnp.pad(dt_bias, (0, pad)).reshape(1, LANES)
    alog = jnp.pad(a_log, (0, pad)).reshape(1, LANES)
    dskip = jnp.repeat(d_skip, SSM_HEAD_DIM).reshape(1, D_INNER)
    expand = (np.arange(LANES)[:, None] == (np.arange(D_INNER)[None, :] // SSM_HEAD_DIM))
    expand = jnp.asarray(expand, dtype=BF16)
    tok = lambda w: pl.BlockSpec((CHUNK, w), lambda b, c: (b * nchunk + c, 0))
    const = lambda r, w: pl.BlockSpec((r, w), lambda b, c: (0, 0))
    return pl.pallas_call(
        _ssm_kernel,
        grid=(batch, nchunk),
        in_specs=[tok(D_INNER), tok(CONV_DIM), tok(LANES),
                  const(CONV_WIDTH, CONV_DIM), const(1, CONV_DIM), const(1, LANES),
                  const(1, LANES), const(1, D_INNER), const(1, D_INNER),
                  const(LANES, D_INNER)],
        out_specs=tok(D_INNER),
        out_shape=jax.ShapeDtypeStruct((t, D_INNER), BF16),
        scratch_shapes=[pltpu.VMEM((SUBLANES, CONV_DIM), F32),
                        pltpu.VMEM((SSM_GROUPS, D_STATE, D_INNER // SSM_GROUPS), F32)],
        compiler_params=_params(("parallel", "arbitrary")),
        name="ssm_branch",
    )(z, xbc, dt_slab, convw_t, conv_b.reshape(1, CONV_DIM), dtb, alog, dskip,
      norm_g.reshape(1, D_INNER), expand)


def _rope_kernel(pos_ref, inva_ref, invi_ref, ca_ref, sa_ref, ci_ref, si_ref):
    pos = pos_ref[...]
    ang_a = inva_ref[...] * pos
    ang_i = invi_ref[...] * pos
    ca_ref[...] = jnp.cos(ang_a)
    sa_ref[...] = jnp.sin(ang_a)
    ci_ref[...] = jnp.cos(ang_i)
    si_ref[...] = jnp.sin(ang_i)


def rope_tables_t(positions, tile=2048):
    t = positions.size
    tile = min(tile, t)
    pos = positions.reshape(1, t).astype(F32)
    ha, hi = ATT_HEAD_DIM // 2, IDX_DIM // 2
    inv_a = (1.0 / (ROPE_THETA ** (jnp.arange(0, ATT_HEAD_DIM, 2, dtype=F32) / ATT_HEAD_DIM))).reshape(ha, 1)
    inv_i = (1.0 / (ROPE_THETA ** (jnp.arange(0, IDX_DIM, 2, dtype=F32) / IDX_DIM))).reshape(hi, 1)
    col = lambda r: pl.BlockSpec((r, tile), lambda i: (0, i))
    return pl.pallas_call(
        _rope_kernel,
        grid=(t // tile,),
        in_specs=[col(1), pl.BlockSpec((ha, 1), lambda i: (0, 0)), pl.BlockSpec((hi, 1), lambda i: (0, 0))],
        out_specs=[col(ha), col(ha), col(hi), col(hi)],
        out_shape=[jax.ShapeDtypeStruct((ha, t), F32)] * 2 + [jax.ShapeDtypeStruct((hi, t), F32)] * 2,
        compiler_params=_params(("parallel",)),
        name="rope_tables",
    )(pos, inv_a, inv_i)


_Q0 = 0
_K0 = _Q0 + ATT_WIDTH
_V0 = _K0 + ATT_KV_HEADS * ATT_HEAD_DIM
_QI0 = _V0 + ATT_KV_HEADS * ATT_HEAD_DIM
_KI0 = _QI0 + IDX_HEADS * IDX_DIM
_WI0 = _KI0 + IDX_DIM
_ATT_ROWS = _WI0 + IDX_HEADS


def _rope_rows(x, cos, sin):
    half = x.shape[0] // 2
    x1, x2 = x[:half], x[half:]
    return jnp.concatenate([x1 * cos - x2 * sin, x1 * sin + x2 * cos], axis=0)


def _head_norm_rows(x, g):
    return x * lax.rsqrt(jnp.mean(x * x, axis=0, keepdims=True) + EPS) * g


def _dsa_prep_kernel(p_ref, ca_ref, sa_ref, ci_ref, si_ref, qg_ref, kg_ref,
                     qt_ref, k_ref, v_ref, qit_ref, ki_ref, wit_ref):
    tile = p_ref.shape[1]
    ca, sa, ci, si = ca_ref[...], sa_ref[...], ci_ref[...], si_ref[...]
    hd = ATT_HEAD_DIM
    for h in range(ATT_HEADS):
        x = _head_norm_rows(p_ref[_Q0 + h * hd:_Q0 + (h + 1) * hd, :], qg_ref[...])
        qt_ref[h * hd:(h + 1) * hd, :] = (_rope_rows(x, ca, sa) * (hd ** -0.5)).astype(BF16)
    for h in range(ATT_KV_HEADS):
        x = _head_norm_rows(p_ref[_K0 + h * hd:_K0 + (h + 1) * hd, :], kg_ref[...])
        k_ref[:, h * hd:(h + 1) * hd] = _rope_rows(x, ca, sa).T.astype(BF16)
    for s in range(tile // KEY_CHUNK):
        v_ref[s] = p_ref[_V0:_V0 + ATT_KV_HEADS * hd, s * KEY_CHUNK:(s + 1) * KEY_CHUNK].astype(BF16)
    for h in range(IDX_HEADS):
        x = _rope_rows(p_ref[_QI0 + h * IDX_DIM:_QI0 + (h + 1) * IDX_DIM, :], ci, si) * (IDX_DIM ** -0.5)
        x = x.astype(BF16)
        for s in range(tile // Q_BLOCK):
            qit_ref[s, :, h * Q_BLOCK:(h + 1) * Q_BLOCK] = x[:, s * Q_BLOCK:(s + 1) * Q_BLOCK]
    x = _rope_rows(p_ref[_KI0:_KI0 + IDX_DIM, :], ci, si)
    zpad = jnp.zeros((LANES - IDX_DIM, tile), F32)
    ki_ref[...] = jnp.concatenate([x, zpad], axis=0).T[:, :IDX_DIM].astype(BF16)
    wit_ref[...] = p_ref[_WI0:_WI0 + IDX_HEADS, :] * (IDX_HEADS ** -0.5)


def dsa_prep(proj_t, tables, q_norm, k_norm, tile=512):
    t = proj_t.shape[1]
    tile = min(tile, t)
    ca, sa, ci, si = tables
    ha, hi = ATT_HEAD_DIM // 2, IDX_DIM // 2
    col = lambda r: pl.BlockSpec((r, tile), lambda i: (0, i))
    rowb = lambda w: pl.BlockSpec((tile, w), lambda i: (i, 0))
    const = lambda r, w: pl.BlockSpec((r, w), lambda i: (0, 0))
    kvw = ATT_KV_HEADS * ATT_HEAD_DIM
    return pl.pallas_call(
        _dsa_prep_kernel,
        grid=(t // tile,),
        in_specs=[col(_ATT_ROWS), col(ha), col(ha), col(hi), col(hi),
                  const(ATT_HEAD_DIM, 1), const(ATT_HEAD_DIM, 1)],
        out_specs=[col(ATT_WIDTH), rowb(kvw),
                   pl.BlockSpec((tile // KEY_CHUNK, kvw, KEY_CHUNK), lambda i: (i, 0, 0)),
                   pl.BlockSpec((tile // Q_BLOCK, IDX_DIM, IDX_HEADS * Q_BLOCK), lambda i: (i, 0, 0)),
                   rowb(IDX_DIM), col(IDX_HEADS)],
        out_shape=[jax.ShapeDtypeStruct((ATT_WIDTH, t), BF16),
                   jax.ShapeDtypeStruct((t, kvw), BF16),
                   jax.ShapeDtypeStruct((t // KEY_CHUNK, kvw, KEY_CHUNK), BF16),
                   jax.ShapeDtypeStruct((t // Q_BLOCK, IDX_DIM, IDX_HEADS * Q_BLOCK), BF16),
                   jax.ShapeDtypeStruct((t, IDX_DIM), BF16),
                   jax.ShapeDtypeStruct((IDX_HEADS, t), F32)],
        compiler_params=_params(("parallel",)),
        name="dsa_prep",
    )(proj_t, ca, sa, ci, si, q_norm.reshape(ATT_HEAD_DIM, 1), k_norm.reshape(ATT_HEAD_DIM, 1))


def _dsa_kernel(ki_ref, k_ref, v_ref, qit_ref, qt_ref, wit_ref, o_ref, keys_ref, bias_ref, acc_ref, s_ref,
                *, topk, idx_bits):
    qb = Q_BLOCK
    kc = KEY_CHUNK
    j = pl.program_id(1)
    nk = (j * qb + qb + kc - 1) // kc
    row = lax.broadcasted_iota(I32, (kc, qb), 0)
    col = lax.broadcasted_iota(I32, (kc, qb), 1)
    q_pos = j * qb + col

    def score_chunk(c, carry):
        ks = pl.multiple_of(c * kc, kc)
        s = _dot(ki_ref[pl.ds(ks, kc), :], qit_ref[0])
        acc = jnp.zeros((kc, qb), F32)
        for h in range(IDX_HEADS):
            acc = acc + wit_ref[h:h + 1, :] * jnp.maximum(s[:, h * qb:(h + 1) * qb], 0.0)
        acc = jnp.where(acc == 0.0, 0.0, acc)
        bits = pltpu.bitcast(acc, I32)
        key = jnp.where(bits < 0, bits ^ np.int32(0x7FFFFFFF), bits)
        keys_ref[pl.ds(ks, kc), :] = jnp.where(ks + row <= q_pos, key, INT_MIN)
        return carry

    lax.fori_loop(0, nk, score_chunk, 0)

    def count(hit):
        def body(c, acc):
            ks = pl.multiple_of(c * kc, kc)
            one = hit(keys_ref[pl.ds(ks, kc), :], ks + row)
            return acc + one.reshape(kc // SUBLANES, SUBLANES, qb).sum(axis=0)
        acc = lax.fori_loop(0, nk, body, jnp.zeros((SUBLANES, qb), I32))
        return acc.sum(axis=0, keepdims=True)

    def value_bit(i, prefix):
        cand = prefix | jnp.left_shift(np.int32(1), 31 - i)
        cand_s = cand ^ INT_MIN
        n = count(lambda key, idx: jnp.where(key >= cand_s, 1, 0))
        return jnp.where(n >= topk, cand, prefix)

    prefix = lax.fori_loop(0, 32, value_bit, jnp.zeros((1, qb), I32))
    thr = prefix ^ INT_MIN

    n_gt = count(lambda key, idx: jnp.where(key > thr, 1, 0))
    n_ge = count(lambda key, idx: jnp.where(key >= thr, 1, 0))
    need = topk - n_gt
    excess = jnp.max(jnp.where(thr > INT_MIN, n_ge - n_gt - need, 0))

    def tie_search():
        def index_bit(i, prefix):
            cand = prefix | jnp.left_shift(np.int32(1), idx_bits - 1 - i)
            n = count(lambda key, idx: jnp.where(key == thr, jnp.where(idx < cand, 1, 0), 0))
            return jnp.where(n < need, cand, prefix)
        return lax.fori_loop(0, idx_bits, index_bit, jnp.zeros((1, qb), I32))

    last = lax.cond(excess > 0, tie_search, lambda: jnp.full((1, qb), 2 ** idx_bits, I32))

    def bias_chunk(c, carry):
        ks = pl.multiple_of(c * kc, kc)
        key = keys_ref[pl.ds(ks, kc), :]
        idx = ks + row
        tied = jnp.where(key == thr, jnp.where(idx <= last, 0.0, NEG_BIG), NEG_BIG)
        bias = jnp.where(key > thr, 0.0, tied)
        bias_ref[pl.ds(ks, kc), :] = jnp.where(idx <= q_pos, bias, NEG_BIG)
        return carry

    lax.fori_loop(0, nk, bias_chunk, 0)

    rep = ATT_HEADS // ATT_KV_HEADS
    hd = ATT_HEAD_DIM
    wide = rep * qb
    groups = range(ATT_KV_HEADS)
    qgs = [jnp.concatenate([qt_ref[(g * rep + r) * hd:(g * rep + r + 1) * hd, :] for r in range(rep)], axis=1)
           for g in groups]

    def fold(x, op):
        return op(x.reshape(kc // SUBLANES, SUBLANES, wide), axis=0)

    def score_sweep(c, carry):
        ks = pl.multiple_of(c * kc, kc)
        b = bias_ref[pl.ds(ks, kc), :]
        b = jnp.concatenate([b] * rep, axis=1)
        out = []
        for g in groups:
            s = _dot(k_ref[pl.ds(ks, kc), g * hd:(g + 1) * hd], qgs[g]) + b
            s_ref[g, pl.ds(ks, kc), :] = s
            out.append(jnp.maximum(carry[g], fold(s, jnp.max)))
        return tuple(out)

    tops = lax.fori_loop(0, nk, score_sweep, tuple(jnp.full((SUBLANES, wide), NEG_BIG, F32) for _ in groups))
    tops = [jnp.max(m, axis=0, keepdims=True) for m in tops]
    acc_ref[...] = jnp.zeros_like(acc_ref)

    def value_sweep(c, carry):
        ks = pl.multiple_of(c * kc, kc)
        out = []
        for g in groups:
            p = jnp.exp(s_ref[g, pl.ds(ks, kc), :] - tops[g])
            acc_ref[g] += _dot(v_ref[c, g * hd:(g + 1) * hd, :], p.astype(BF16))
            out.append(carry[g] + fold(p, jnp.sum))
        return tuple(out)

    sums = lax.fori_loop(0, nk, value_sweep, tuple(jnp.zeros((SUBLANES, wide), F32) for _ in groups))
    for g in groups:
        o = acc_ref[g] / jnp.sum(sums[g], axis=0, keepdims=True)
        for r in range(rep):
            h = g * rep + r
            o_ref[:, h * hd:(h + 1) * hd] = o[:, r * qb:(r + 1) * qb].T.astype(o_ref.dtype)


def dsa_attention(ki, k, v3, qit, qt, wit, batch, seq):
    t = batch * seq
    nqb = seq // Q_BLOCK
    kvw = ATT_KV_HEADS * ATT_HEAD_DIM
    topk = min(TOPK_MAX, seq // 4)
    idx_bits = max(1, int(math.ceil(math.log2(seq))))
    assert seq % KEY_CHUNK == 0 and KEY_CHUNK % Q_BLOCK == 0
    qcol = lambda r: pl.BlockSpec((r, Q_BLOCK), lambda b, j: (0, b * nqb + j))
    return pl.pallas_call(
        functools.partial(_dsa_kernel, topk=topk, idx_bits=idx_bits),
        grid=(batch, nqb),
        in_specs=[pl.BlockSpec((seq, IDX_DIM), lambda b, j: (b, 0)),
                  pl.BlockSpec((seq, kvw), lambda b, j: (b, 0)),
                  pl.BlockSpec((seq // KEY_CHUNK, kvw, KEY_CHUNK), lambda b, j: (b, 0, 0)),
                  pl.BlockSpec((1, IDX_DIM, IDX_HEADS * Q_BLOCK), lambda b, j: (b * nqb + j, 0, 0)),
                  qcol(ATT_WIDTH), qcol(IDX_HEADS)],
        out_specs=pl.BlockSpec((Q_BLOCK, ATT_WIDTH), lambda b, j: (b * nqb + j, 0)),
        out_shape=jax.ShapeDtypeStruct((t, ATT_WIDTH), BF16),
        scratch_shapes=[pltpu.VMEM((seq, Q_BLOCK), I32), pltpu.VMEM((seq, Q_BLOCK), F32),
                        pltpu.VMEM((ATT_KV_HEADS, ATT_HEAD_DIM, (ATT_HEADS // ATT_KV_HEADS) * Q_BLOCK), F32),
                        pltpu.VMEM((ATT_KV_HEADS, seq, (ATT_HEADS // ATT_KV_HEADS) * Q_BLOCK), F32)],
        compiler_params=_params(("parallel", "arbitrary")),
        name="dsa_attention",
    )(ki, k, v3, qit, qt, wit)


def _merge_kernel(x_ref, ys_ref, ya_ref, gate_ref, wbs_ref, wba_ref, wo_ref, o_ref):
    d = x_ref.shape[1]
    gates = gate_ref[...]
    merged = (_sigmoid(gates[:, :d]) * _dot(ys_ref[...], wbs_ref[...])
              + _sigmoid(gates[:, d:]) * _dot(ya_ref[...], wba_ref[...]))
    o_ref[...] = x_ref[...] + _dot(merged.astype(BF16), wo_ref[...])


def merge_branches(x, y_ssm, y_att, gates, w_bs, w_ba, w_o, tm=512):
    t, d = x.shape
    tm = min(tm, t)
    tok = lambda w: pl.BlockSpec((tm, w), lambda i: (i, 0))
    const = lambda r, w: pl.BlockSpec((r, w), lambda i: (0, 0))
    return pl.pallas_call(
        _merge_kernel,
        grid=(t // tm,),
        in_specs=[tok(d), tok(D_INNER), tok(ATT_WIDTH), tok(2 * d),
                  const(D_INNER, d), const(ATT_WIDTH, d), const(d, d)],
        out_specs=tok(d),
        out_shape=jax.ShapeDtypeStruct((t, d), F32),
        compiler_params=_params(("parallel",)),
        name="merge_branches",
    )(x, y_ssm, y_att, gates, w_bs, w_ba, w_o)


def _top_rows(s, k, rows_out):
    rows = lax.broadcasted_iota(I32, (rows_out, s.shape[1]), 0)

    def body(r, carry):
        s, top = carry
        m = jnp.max(s, axis=0, keepdims=True)
        return jnp.where(s == m, -jnp.inf, s), jnp.where(rows == r, m, top)

    _, top = lax.fori_loop(0, k, body, (s, jnp.full((rows_out, s.shape[1]), -jnp.inf, F32)))
    return top


def _peer_route_kernel(x_ref, g_ref, wqt_ref, keys_ref, ht_ref, tau_ref, e1_ref, e2_ref):
    tm = x_ref.shape[0]
    h = _rms(x_ref[...], g_ref[...])
    ht_ref[...] = h.T.astype(BF16)
    qt = lax.dot_general(wqt_ref[...], h.astype(BF16), _NT, preferred_element_type=F32).astype(BF16)
    half = PEER_QDIM // 2
    k = PEER_TOPK
    k1 = k + 1
    rows_out = -(-k1 // SUBLANES) * SUBLANES
    sub8 = lax.broadcasted_iota(I32, (SUBLANES, tm), 0)
    for hh in range(PEER_HEADS):
        s1 = _dot(keys_ref[2 * hh], qt[(2 * hh) * half:(2 * hh + 1) * half, :])
        s2 = _dot(keys_ref[2 * hh + 1], qt[(2 * hh + 1) * half:(2 * hh + 2) * half, :])
        a = _top_rows(s1, k1, rows_out)
        b = _top_rows(s2, k1, rows_out)
        tiles = [a[r:r + SUBLANES] + b[0:1] for r in range(0, rows_out, SUBLANES)]
        for jj in range(1, k1):
            lim = k1 // (jj + 1)
            tile = a[:SUBLANES] + b[jj:jj + 1]
            tiles.append(tile if lim >= SUBLANES else jnp.where(sub8 < lim, tile, -jnp.inf))
        cand = jnp.concatenate(tiles, axis=0)
        best = a[0:1] + b[0:1]

        def body(r, carry):
            cand, zsum, kth, nxt = carry
            m = jnp.max(cand, axis=0, keepdims=True)
            zsum = zsum + jnp.where(r < k, jnp.exp(m - best), 0.0)
            return jnp.where(cand == m, -jnp.inf, cand), zsum, jnp.where(r == k - 1, m, kth), m

        _, zsum, kth, nxt = lax.fori_loop(0, k1, body, (cand, jnp.zeros((1, tm), F32), best, best))
        cut = 0.5 * (kth + nxt)
        r0 = hh * N_KEYS
        tau_ref[r0:r0 + N_KEYS, :] = jnp.exp((cut - b[0:1]) - s1)
        e1_ref[r0:r0 + N_KEYS, :] = jnp.exp(s1 - a[0:1]) / zsum
        e2_ref[r0:r0 + N_KEYS, :] = jnp.exp(s2 - b[0:1])


def peer_route(x, g, wq_t, keys, tm=256):
    t, d = x.shape
    tm = min(tm, t)
    rows = PEER_HEADS * N_KEYS
    col = lambda r: pl.BlockSpec((r, tm), lambda i: (0, i))
    return pl.pallas_call(
        _peer_route_kernel,
        grid=(t // tm,),
        in_specs=[pl.BlockSpec((tm, d), lambda i: (i, 0)),
                  pl.BlockSpec((1, d), lambda i: (0, 0)),
                  pl.BlockSpec(wq_t.shape, lambda i: (0, 0)),
                  pl.BlockSpec(keys.shape, lambda i: (0, 0, 0))],
        out_specs=[col(d), col(rows), col(rows), col(rows)],
        out_shape=[jax.ShapeDtypeStruct((d, t), BF16)] + [jax.ShapeDtypeStruct((rows, t), F32)] * 3,
        compiler_params=_params(("parallel",)),
        name="peer_route",
    )(x, g, wq_t, keys)


PEER_ROWS = 32


def _peer_dense_kernel(x_ref, ht_ref, tau_ref, e1_ref, e2_ref, u_ref, vt_ref, o_ref, acc_ref, hid_ref, act_ref,
                       *, ib, nblk):
    j = pl.program_id(1)
    tt = ht_ref.shape[1]
    cur = j % 2
    prev = 1 - cur

    @pl.when(j == 0)
    def _():
        acc_ref[...] = jnp.zeros_like(acc_ref)
        hid_ref[...] = jnp.zeros_like(hid_ref)
        act_ref[...] = jnp.zeros_like(act_ref)

    acc_ref[...] += _dot(vt_ref[...], act_ref[cur])
    blk = jnp.clip(j - 1, 0, nblk - 1)
    for ii in range(ib):
        i1 = blk * ib + ii
        taus = [tau_ref[pl.ds(hh * N_KEYS + i1, 1), :] for hh in range(PEER_HEADS)]
        e1s = [e1_ref[pl.ds(hh * N_KEYS + i1, 1), :] for hh in range(PEER_HEADS)]
        for lc in range(tt // LANES):
            cols = slice(lc * LANES, (lc + 1) * LANES)
            tau_b = [jnp.broadcast_to(taus[hh][:, cols], (PEER_ROWS, LANES)) for hh in range(PEER_HEADS)]
            e1_b = [jnp.broadcast_to(e1s[hh][:, cols], (PEER_ROWS, LANES)) for hh in range(PEER_HEADS)]
            for rb in range(N_KEYS // PEER_ROWS):
                gate = jnp.zeros((PEER_ROWS, LANES), F32)
                for hh in range(PEER_HEADS):
                    r0 = hh * N_KEYS + rb * PEER_ROWS
                    e2 = e2_ref[r0:r0 + PEER_ROWS, cols]
                    gate = gate + jnp.where(e2 >= tau_b[hh], e2, 0.0) * e1_b[hh]
                rows = slice(ii * N_KEYS + rb * PEER_ROWS, ii * N_KEYS + (rb + 1) * PEER_ROWS)
                hb = hid_ref[prev, rows, cols]
                act = 0.5 * hb * (1.0 + lax.erf(hb * np.float32(math.sqrt(0.5))))
                act_ref[prev, rows, cols] = (act * gate).astype(BF16)
    hid_ref[cur] = _dot(u_ref[...], ht_ref[...])

    @pl.when(j == pl.num_programs(1) - 1)
    def _():
        o_ref[...] = x_ref[...] + acc_ref[...].T


def peer_dense(x, h_t, tau, e1, e2, u, v_t, tt=512, ib=4):
    t, d = x.shape
    tt = min(tt, t)
    rows = PEER_HEADS * N_KEYS
    nblk = N_KEYS // ib
    col = lambda r: pl.BlockSpec((r, tt), lambda i, j: (0, i))
    tok = pl.BlockSpec((tt, d), lambda i, j: (i, 0))
    return pl.pallas_call(
        functools.partial(_peer_dense_kernel, ib=ib, nblk=nblk),
        grid=(t // tt, nblk + 2),
        in_specs=[tok, col(d), col(rows), col(rows), col(rows),
                  pl.BlockSpec((ib * N_KEYS, d), lambda i, j: (jnp.minimum(j, nblk - 1), 0)),
                  pl.BlockSpec((d, ib * N_KEYS), lambda i, j: (0, jnp.clip(j - 2, 0, nblk - 1)))],
        out_specs=tok,
        out_shape=jax.ShapeDtypeStruct((t, d), F32),
        scratch_shapes=[pltpu.VMEM((d, tt), F32), pltpu.VMEM((2, ib * N_KEYS, tt), F32),
                        pltpu.VMEM((2, ib * N_KEYS, tt), BF16)],
        compiler_params=_params(("parallel", "arbitrary")),
        name="peer_dense",
    )(x, h_t, tau, e1, e2, u, v_t)


def _ple_kernel(x_ref, p_ref, g_ref, wg_ref, wp_ref, o_ref):
    x = x_ref[...]
    h = _rms(x, g_ref[...]).astype(BF16)
    gate = _sigmoid(_dot(h, wg_ref[...]))
    o_ref[...] = x + gate * _dot(p_ref[...].astype(BF16), wp_ref[...])


def ple_block(x, p, g, w_gate, w_proj, tm=512):
    t, d = x.shape
    tm = min(tm, t)
    pd = p.shape[1]
    return pl.pallas_call(
        _ple_kernel,
        grid=(t // tm,),
        in_specs=[pl.BlockSpec((tm, d), lambda i: (i, 0)),
                  pl.BlockSpec((tm, pd), lambda i: (i, 0)),
                  pl.BlockSpec((1, d), lambda i: (0, 0)),
                  pl.BlockSpec((d, d), lambda i: (0, 0)),
                  pl.BlockSpec((pd, d), lambda i: (0, 0))],
        out_specs=pl.BlockSpec((tm, d), lambda i: (i, 0)),
        out_shape=jax.ShapeDtypeStruct((t, d), F32),
        compiler_params=_params(("parallel",)),
        name="ple_block",
    )(x, p, g, w_gate, w_proj)


def _split_w_in(w):
    sizes = [D_INNER, CONV_DIM, SSM_HEADS, ATT_WIDTH, ATT_KV_HEADS * ATT_HEAD_DIM,
             ATT_KV_HEADS * ATT_HEAD_DIM, IDX_HEADS * IDX_DIM, IDX_DIM, IDX_HEADS, 2 * w.shape[0]]
    out, acc = [], 0
    for s in sizes:
        out.append(w[:, acc:acc + s])
        acc += s
    return out


def kernel(x, p, positions, norm_mix, w_in, conv_w, conv_b, dt_bias, a_log, d_skip, ssm_norm, q_norm, k_norm,
           w_branch, w_out, norm_ffn, peer_wq, peer_keys, peer_u, peer_v, norm_ple, w_ple_gate, w_ple_proj):
    batch, seq, d = x.shape
    t = batch * seq
    depth = w_in.shape[0]
    tables = rope_tables_t(positions)
    xf = x.reshape(t, d)
    for i in range(depth):
        wz, wxbc, wdt, wq, wk, wv, wqi, wki, wwi, wg = _split_w_in(w_in[i])
        w_dt = jnp.pad(wdt, ((0, 0), (0, LANES - SSM_HEADS))).astype(BF16)
        w_att_t = jnp.transpose(jnp.concatenate([wq, wk, wv, wqi, wki, wwi], axis=1)).astype(BF16)
        g_mix = norm_mix[i].reshape(1, d)

        z = norm_matmul(xf, g_mix, wz.astype(BF16))
        xbc = norm_matmul(xf, g_mix, wxbc.astype(BF16))
        dt_slab = norm_matmul(xf, g_mix, w_dt)
        gates = norm_matmul(xf, g_mix, wg.astype(BF16))
        proj_t = norm_matmul_t(xf, g_mix, w_att_t)

        y_ssm = ssm_branch(z, xbc, dt_slab, conv_w[i], conv_b[i], dt_bias[i], a_log[i], d_skip[i],
                           ssm_norm[i], batch, seq)
        qt, k, v3, qit, ki, wit = dsa_prep(proj_t, tables, q_norm[i], k_norm[i])
        y_att = dsa_attention(ki, k, v3, qit, qt, wit, batch, seq)
        xf = merge_branches(xf, y_ssm, y_att, gates, w_branch[i, :D_INNER].astype(BF16),
                            w_branch[i, D_INNER:].astype(BF16), w_out[i].astype(BF16))

        keys = peer_keys[i].reshape(PEER_HEADS * 2, N_KEYS, PEER_QDIM // 2).astype(BF16)
        h_t, tau, e1, e2 = peer_route(xf, norm_ffn[i].reshape(1, d), jnp.transpose(peer_wq[i]).astype(BF16), keys)
        xf = peer_dense(xf, h_t, tau, e1, e2, peer_u[i].astype(BF16), jnp.transpose(peer_v[i]).astype(BF16))

        xf = ple_block(xf, p[i].reshape(t, -1), norm_ple[i].reshape(1, d),
                       w_ple_gate[i].astype(BF16), w_ple_proj[i].astype(BF16))
    return xf.reshape(batch, seq, d)
```

```python
import functools
import math

import numpy as np
import jax
import jax.numpy as jnp
from jax import lax
from jax.experimental import pallas as pl
from jax.experimental.pallas import tpu as pltpu

F32 = jnp.float32
BF16 = jnp.bfloat16
I32 = jnp.int32

EPS = 1e-6
ROPE_THETA = 10000.0

SSM_HEADS = 32
SSM_HEAD_DIM = 64
D_INNER = SSM_HEADS * SSM_HEAD_DIM
SSM_GROUPS = 4
D_STATE = 128
CONV_WIDTH = 4
CHUNK = 128
CONV_DIM = D_INNER + 2 * SSM_GROUPS * D_STATE
ATT_HEADS = 8
ATT_KV_HEADS = 2
ATT_HEAD_DIM = 128
ATT_WIDTH = ATT_HEADS * ATT_HEAD_DIM
IDX_HEADS = 8
IDX_DIM = 64
TOPK_MAX = 256
Q_BLOCK = 128
KEY_CHUNK = 256
PEER_HEADS = 8
N_KEYS = 128
PEER_TOPK = 16
PEER_QDIM = 256

LANES = 128
SUBLANES = 8
VMEM_LIMIT = 56 * 1024 * 1024
PACKED_ROWS = 16
INT_MIN = np.int32(-2 ** 31)
MIN16 = np.int16(-2 ** 15)
I16 = jnp.int16
NEG_BIG = -1e30

_NT = (((1,), (1,)), ((), ()))


def _params(sem):
    return pltpu.CompilerParams(dimension_semantics=sem, vmem_limit_bytes=VMEM_LIMIT)


def _rms(x, g):
    return x * lax.rsqrt(jnp.mean(x * x, axis=-1, keepdims=True) + EPS) * g


def _sigmoid(x):
    return 1.0 / (1.0 + jnp.exp(-x))


def _split3(a):
    a1 = a.astype(BF16)
    r1 = a - a1.astype(F32)
    a2 = r1.astype(BF16)
    a3 = (r1 - a2.astype(F32)).astype(BF16)
    return a1, a2, a3


def _dot(a, b):
    return jnp.dot(a, b, preferred_element_type=F32)


def _norm_mm_kernel(x_ref, g_ref, w_ref, o_ref):
    h = _rms(x_ref[...], g_ref[...]).astype(BF16)
    o_ref[...] = _dot(h, w_ref[...]).astype(o_ref.dtype)


def norm_matmul(x, g, w, out_dtype=F32, tm=512):
    m, k = x.shape
    n = w.shape[1]
    return pl.pallas_call(
        _norm_mm_kernel,
        grid=(m // tm,),
        in_specs=[pl.BlockSpec((tm, k), lambda i: (i, 0)),
                  pl.BlockSpec((1, k), lambda i: (0, 0)),
                  pl.BlockSpec((k, n), lambda i: (0, 0))],
        out_specs=pl.BlockSpec((tm, n), lambda i: (i, 0)),
        out_shape=jax.ShapeDtypeStruct((m, n), out_dtype),
        compiler_params=_params(("parallel",)),
        name="norm_matmul",
    )(x, g, w)


def _norm_mm_t_kernel(x_ref, g_ref, wt_ref, o_ref):
    h = _rms(x_ref[...], g_ref[...]).astype(BF16)
    o_ref[...] = lax.dot_general(wt_ref[...], h, _NT, preferred_element_type=F32)


def norm_matmul_t(x, g, wt, tm=512):
    m, k = x.shape
    n = wt.shape[0]
    return pl.pallas_call(
        _norm_mm_t_kernel,
        grid=(m // tm,),
        in_specs=[pl.BlockSpec((tm, k), lambda i: (i, 0)),
                  pl.BlockSpec((1, k), lambda i: (0, 0)),
                  pl.BlockSpec((n, k), lambda i: (0, 0))],
        out_specs=pl.BlockSpec((n, tm), lambda i: (0, i)),
        out_shape=jax.ShapeDtypeStruct((n, m), F32),
        compiler_params=_params(("parallel",)),
        name="norm_matmul_t",
    )(x, g, wt)


def _ssm_kernel(z_ref, xbc_ref, dt_ref, convw_ref, convb_ref, dtb_ref, alog_ref,
                dskip_ref, normg_ref, expand_ref, y_ref, carry_ref, state_ref):
    @pl.when(pl.program_id(1) == 0)
    def _():
        carry_ref[...] = jnp.zeros_like(carry_ref)
        state_ref[...] = jnp.zeros_like(state_ref)

    xbc = xbc_ref[...]
    ext = jnp.concatenate([carry_ref[...], xbc], axis=0)
    conv = convb_ref[...]
    for j in range(CONV_WIDTH):
        lo = SUBLANES - (CONV_WIDTH - 1) + j
        conv = conv + convw_ref[j:j + 1, :] * ext[lo:lo + CHUNK]
    carry_ref[...] = xbc[CHUNK - SUBLANES:]
    xc = conv * _sigmoid(conv)
    xs = xc[:, :D_INNER]
    gw = SSM_GROUPS * D_STATE
    bm = xc[:, D_INNER:D_INNER + gw]
    cm = xc[:, D_INNER + gw:]

    lane = lax.broadcasted_iota(I32, (CHUNK, LANES), 1)
    row = lax.broadcasted_iota(I32, (CHUNK, LANES), 0)
    head_lane = lane < SSM_HEADS
    dt = jax.nn.softplus(dt_ref[...] + dtb_ref[...])
    adt = jnp.where(head_lane, dt * (-jnp.exp(alog_ref[...])), 0.0)
    tril = jnp.where(row >= lane, 1.0, 0.0).astype(BF16)
    cs = sum(_dot(tril, part) for part in _split3(adt))
    cs_row = cs.T
    cs_last = cs[CHUNK - 1:CHUNK, :]
    ecs = jnp.exp(cs)
    decay = jnp.exp(cs_last - cs)

    expand = expand_ref[...]

    def widen(a):
        return sum(_dot(part, expand) for part in _split3(a))

    dt_w = widen(dt)
    ecs_w = widen(ecs)
    decay_w = widen(decay)
    xdt = xs * dt_w
    xdt_b = xdt.astype(BF16)
    xdec_b = (xdt * decay_w).astype(BF16)
    causal = row >= lane
    pair_lo = lane < SSM_HEAD_DIM

    gh = SSM_HEADS // SSM_GROUPS
    gcols = gh * SSM_HEAD_DIM
    ys = []
    for g in range(SSM_GROUPS):
        bg = bm[:, g * D_STATE:(g + 1) * D_STATE]
        cg = cm[:, g * D_STATE:(g + 1) * D_STATE].astype(BF16)
        cb = lax.dot_general(cg, bg.astype(BF16), _NT, preferred_element_type=F32)
        st = state_ref[g]
        y_off = _dot(cg, st.astype(BF16))
        new_st = _dot(bg.T.astype(BF16), xdec_b[:, g * gcols:(g + 1) * gcols])
        state_ref[g] = st * ecs_w[CHUNK - 1:CHUNK, g * gcols:(g + 1) * gcols] + new_st
        pieces = []
        for pr in range(gh // 2):
            outs = []
            for sub in range(2):
                h = g * gh + pr * 2 + sub
                lm = jnp.where(causal, jnp.exp(cs[:, h:h + 1] - cs_row[h:h + 1, :]), 0.0)
                mm = (cb * lm).astype(BF16)
                c0 = (h // 2) * LANES
                outs.append(_dot(mm, xdt_b[:, c0:c0 + LANES]))
            pieces.append(jnp.where(pair_lo, outs[0], outs[1]))
        y_diag = jnp.concatenate(pieces, axis=1)
        ys.append(y_diag + y_off * ecs_w[:, g * gcols:(g + 1) * gcols])
    y = jnp.concatenate(ys, axis=1) + dskip_ref[...] * xs
    z = z_ref[...]
    y = y * (z * _sigmoid(z))
    outs = []
    for g in range(SSM_GROUPS):
        yg = y[:, g * gcols:(g + 1) * gcols]
        outs.append(yg * lax.rsqrt(jnp.mean(yg * yg, axis=-1, keepdims=True) + EPS))
    y_ref[...] = (jnp.concatenate(outs, axis=1) * normg_ref[...]).astype(y_ref.dtype)


def ssm_branch(z, xbc, dt_slab, conv_w, conv_b, dt_bias, a_log, d_skip, norm_g, batch, seq):
    t = batch * seq
    nchunk = seq // CHUNK
    pad = LANES - SSM_HEADS
    convw_t = jnp.transpose(conv_w)
    dtb = jnp.pad(dt_bias, (0, pad)).reshape(1, LANES)
    alog = jnp.pad(a_log, (0, pad)).reshape(1, LANES)
    dskip = jnp.repeat(d_skip, SSM_HEAD_DIM).reshape(1, D_INNER)
    expand = (np.arange(LANES)[:, None] == (np.arange(D_INNER)[None, :] // SSM_HEAD_DIM))
    expand = jnp.asarray(expand, dtype=BF16)
    tok = lambda w: pl.BlockSpec((CHUNK, w), lambda b, c: (b * nchunk + c, 0))
    const = lambda r, w: pl.BlockSpec((r, w), lambda b, c: (0, 0))
    return pl.pallas_call(
        _ssm_kernel,
        grid=(batch, nchunk),
        in_specs=[tok(D_INNER), tok(CONV_DIM), tok(LANES),
                  const(CONV_WIDTH, CONV_DIM), const(1, CONV_DIM), const(1, LANES),
                  const(1, LANES), const(1, D_INNER), const(1, D_INNER),
                  const(LANES, D_INNER)],
        out_specs=tok(D_INNER),
        out_shape=jax.ShapeDtypeStruct((t, D_INNER), BF16),
        scratch_shapes=[pltpu.VMEM((SUBLANES, CONV_DIM), F32),
                        pltpu.VMEM((SSM_GROUPS, D_STATE, D_INNER // SSM_GROUPS), F32)],
        compiler_params=_params(("parallel", "arbitrary")),
        name="ssm_branch",
    )(z, xbc, dt_slab, convw_t, conv_b.reshape(1, CONV_DIM), dtb, alog, dskip,
      norm_g.reshape(1, D_INNER), expand)


def _rope_kernel(pos_ref, inva_ref, invi_ref, ca_ref, sa_ref, ci_ref, si_ref):
    pos = pos_ref[...]
    ang_a = inva_ref[...] * pos
    ang_i = invi_ref[...] * pos
    ca_ref[...] = jnp.cos(ang_a)
    sa_ref[...] = jnp.sin(ang_a)
    ci_ref[...] = jnp.cos(ang_i)
    si_ref[...] = jnp.sin(ang_i)


def rope_tables_t(positions, tile=2048):
    t = positions.size
    tile = min(tile, t)
    pos = positions.reshape(1, t).astype(F32)
    ha, hi = ATT_HEAD_DIM // 2, IDX_DIM // 2
    inv_a = (1.0 / (ROPE_THETA ** (jnp.arange(0, ATT_HEAD_DIM, 2, dtype=F32) / ATT_HEAD_DIM))).reshape(ha, 1)
    inv_i = (1.0 / (ROPE_THETA ** (jnp.arange(0, IDX_DIM, 2, dtype=F32) / IDX_DIM))).reshape(hi, 1)
    col = lambda r: pl.BlockSpec((r, tile), lambda i: (0, i))
    return pl.pallas_call(
        _rope_kernel,
        grid=(t // tile,),
        in_specs=[col(1), pl.BlockSpec((ha, 1), lambda i: (0, 0)), pl.BlockSpec((hi, 1), lambda i: (0, 0))],
        out_specs=[col(ha), col(ha), col(hi), col(hi)],
        out_shape=[jax.ShapeDtypeStruct((ha, t), F32)] * 2 + [jax.ShapeDtypeStruct((hi, t), F32)] * 2,
        compiler_params=_params(("parallel",)),
        name="rope_tables",
    )(pos, inv_a, inv_i)


_Q0 = 0
_K0 = _Q0 + ATT_WIDTH
_V0 = _K0 + ATT_KV_HEADS * ATT_HEAD_DIM
_QI0 = _V0 + ATT_KV_HEADS * ATT_HEAD_DIM
_KI0 = _QI0 + IDX_HEADS * IDX_DIM
_WI0 = _KI0 + IDX_DIM
_ATT_ROWS = _WI0 + IDX_HEADS


def _rope_rows(x, cos, sin):
    half = x.shape[0] // 2
    x1, x2 = x[:half], x[half:]
    return jnp.concatenate([x1 * cos - x2 * sin, x1 * sin + x2 * cos], axis=0)


def _head_norm_rows(x, g):
    return x * lax.rsqrt(jnp.mean(x * x, axis=0, keepdims=True) + EPS) * g


def _dsa_prep_kernel(p_ref, ca_ref, sa_ref, ci_ref, si_ref, qg_ref, kg_ref,
                     qt_ref, k_ref, v_ref, qit_ref, ki_ref, wit_ref):
    tile = p_ref.shape[1]
    ca, sa, ci, si = ca_ref[...], sa_ref[...], ci_ref[...], si_ref[...]
    hd = ATT_HEAD_DIM
    for h in range(ATT_HEADS):
        x = _head_norm_rows(p_ref[_Q0 + h * hd:_Q0 + (h + 1) * hd, :], qg_ref[...])
        qt_ref[h * hd:(h + 1) * hd, :] = (_rope_rows(x, ca, sa) * (hd ** -0.5)).astype(BF16)
    for h in range(ATT_KV_HEADS):
        x = _head_norm_rows(p_ref[_K0 + h * hd:_K0 + (h + 1) * hd, :], kg_ref[...])
        k_ref[:, h * hd:(h + 1) * hd] = _rope_rows(x, ca, sa).T.astype(BF16)
    for s in range(tile // KEY_CHUNK):
        v_ref[s] = p_ref[_V0:_V0 + ATT_KV_HEADS * hd, s * KEY_CHUNK:(s + 1) * KEY_CHUNK].astype(BF16)
    for h in range(IDX_HEADS):
        x = _rope_rows(p_ref[_QI0 + h * IDX_DIM:_QI0 + (h + 1) * IDX_DIM, :], ci, si) * (IDX_DIM ** -0.5)
        x = x.astype(BF16)
        for s in range(tile // Q_BLOCK):
            qit_ref[s, :, h * Q_BLOCK:(h + 1) * Q_BLOCK] = x[:, s * Q_BLOCK:(s + 1) * Q_BLOCK]
    x = _rope_rows(p_ref[_KI0:_KI0 + IDX_DIM, :], ci, si)
    zpad = jnp.zeros((LANES - IDX_DIM, tile), F32)
    ki_ref[...] = jnp.concatenate([x, zpad], axis=0).T[:, :IDX_DIM].astype(BF16)
    wit_ref[...] = p_ref[_WI0:_WI0 + IDX_HEADS, :] * (IDX_HEADS ** -0.5)


def dsa_prep(proj_t, tables, q_norm, k_norm, tile=512):
    t = proj_t.shape[1]
    tile = min(tile, t)
    ca, sa, ci, si = tables
    ha, hi = ATT_HEAD_DIM // 2, IDX_DIM // 2
    col = lambda r: pl.BlockSpec((r, tile), lambda i: (0, i))
    rowb = lambda w: pl.BlockSpec((tile, w), lambda i: (i, 0))
    const = lambda r, w: pl.BlockSpec((r, w), lambda i: (0, 0))
    kvw = ATT_KV_HEADS * ATT_HEAD_DIM
    return pl.pallas_call(
        _dsa_prep_kernel,
        grid=(t // tile,),
        in_specs=[col(_ATT_ROWS), col(ha), col(ha), col(hi), col(hi),
                  const(ATT_HEAD_DIM, 1), const(ATT_HEAD_DIM, 1)],
        out_specs=[col(ATT_WIDTH), rowb(kvw),
                   pl.BlockSpec((tile // KEY_CHUNK, kvw, KEY_CHUNK), lambda i: (i, 0, 0)),
                   pl.BlockSpec((tile // Q_BLOCK, IDX_DIM, IDX_HEADS * Q_BLOCK), lambda i: (i, 0, 0)),
                   rowb(IDX_DIM), col(IDX_HEADS)],
        out_shape=[jax.ShapeDtypeStruct((ATT_WIDTH, t), BF16),
                   jax.ShapeDtypeStruct((t, kvw), BF16),
                   jax.ShapeDtypeStruct((t // KEY_CHUNK, kvw, KEY_CHUNK), BF16),
                   jax.ShapeDtypeStruct((t // Q_BLOCK, IDX_DIM, IDX_HEADS * Q_BLOCK), BF16),
                   jax.ShapeDtypeStruct((t, IDX_DIM), BF16),
                   jax.ShapeDtypeStruct((IDX_HEADS, t), F32)],
        compiler_params=_params(("parallel",)),
        name="dsa_prep",
    )(proj_t, ca, sa, ci, si, q_norm.reshape(ATT_HEAD_DIM, 1), k_norm.reshape(ATT_HEAD_DIM, 1))


def _dsa_kernel(ki_ref, k_ref, v_ref, qit_ref, qt_ref, wit_ref, o_ref, keys_ref, bias_ref, acc_ref, s_ref,
                hi_ref, lo_ref, *, topk, idx_bits):
    qb = Q_BLOCK
    kc = KEY_CHUNK
    j = pl.program_id(1)
    nk = (j * qb + qb + kc - 1) // kc
    row = lax.broadcasted_iota(I32, (kc, qb), 0)
    col = lax.broadcasted_iota(I32, (kc, qb), 1)
    q_pos = j * qb + col

    def score_chunk(c, carry):
        ks = pl.multiple_of(c * kc, kc)
        s = _dot(ki_ref[pl.ds(ks, kc), :], qit_ref[0])
        acc = jnp.zeros((kc, qb), F32)
        for h in range(IDX_HEADS):
            acc = acc + wit_ref[h:h + 1, :] * jnp.maximum(s[:, h * qb:(h + 1) * qb], 0.0)
        acc = jnp.where(acc == 0.0, 0.0, acc)
        bits = pltpu.bitcast(acc, I32)
        key = jnp.where(bits < 0, bits ^ np.int32(0x7FFFFFFF), bits)
        key = jnp.where(ks + row <= q_pos, key, INT_MIN)
        keys_ref[pl.ds(ks, kc), :] = key
        hi_ref[pl.ds(ks, kc), :] = (key >> 16).astype(I16)
        lo_ref[pl.ds(ks, kc), :] = ((key & 0xFFFF) - 2 ** 15).astype(I16)
        return carry

    lax.fori_loop(0, nk, score_chunk, 0)

    @pl.when(nk % 2 == 1)
    def _():
        pad = pl.multiple_of(nk * kc, kc)
        hi_ref[pl.ds(pad, kc), :] = jnp.full((kc, qb), MIN16, I16)
        lo_ref[pl.ds(pad, kc), :] = jnp.full((kc, qb), MIN16, I16)

    nk2 = (nk + 1) // 2
    kc2 = 2 * kc

    def count16(ref, cand, strict=False):
        def body(c, acc):
            ks = pl.multiple_of(c * kc2, kc2)
            x = ref[pl.ds(ks, kc2), :]
            one = jnp.where(x > cand if strict else x >= cand, np.int16(1), np.int16(0))
            parts = [one[r:r + PACKED_ROWS] for r in range(0, kc2, PACKED_ROWS)]
            while len(parts) > 1:
                parts = [parts[i] + parts[i + 1] for i in range(0, len(parts), 2)]
            return acc + parts[0]
        acc = lax.fori_loop(0, nk2, body, jnp.zeros((PACKED_ROWS, qb), I16))
        return acc.astype(I32).sum(axis=0, keepdims=True)

    def search16(ref, want):
        def bit(i, prefix):
            cand = prefix | jnp.left_shift(np.int32(1), 15 - i)
            n = count16(ref, (cand - 2 ** 15).astype(I16))
            return jnp.where(n >= want, cand, prefix)
        return lax.fori_loop(0, 16, bit, jnp.zeros((1, qb), I32))

    def count(hit):
        def body(c, acc):
            ks = pl.multiple_of(c * kc, kc)
            one = hit(keys_ref[pl.ds(ks, kc), :], ks + row)
            return acc + one.reshape(kc // SUBLANES, SUBLANES, qb).sum(axis=0)
        acc = lax.fori_loop(0, nk, body, jnp.zeros((SUBLANES, qb), I32))
        return acc.sum(axis=0, keepdims=True)

    top_hi = search16(hi_ref, topk)
    hi16 = (top_hi - 2 ** 15).astype(I16)
    want_lo = topk - count16(hi_ref, hi16, strict=True)

    def mask_lo(c, carry):
        ks = pl.multiple_of(c * kc2, kc2)
        lo_ref[pl.ds(ks, kc2), :] = jnp.where(hi_ref[pl.ds(ks, kc2), :] == hi16, lo_ref[pl.ds(ks, kc2), :], MIN16)
        return carry

    lax.fori_loop(0, nk2, mask_lo, 0)
    top_lo = search16(lo_ref, want_lo)
    thr = jnp.left_shift(top_hi - 2 ** 15, 16) | top_lo

    n_gt = count(lambda key, idx: jnp.where(key > thr, 1, 0))
    n_ge = count(lambda key, idx: jnp.where(key >= thr, 1, 0))
    need = topk - n_gt
    excess = jnp.max(jnp.where(thr > INT_MIN, n_ge - n_gt - need, 0))

    def tie_search():
        def index_bit(i, prefix):
            cand = prefix | jnp.left_shift(np.int32(1), idx_bits - 1 - i)
            n = count(lambda key, idx: jnp.where(key == thr, jnp.where(idx < cand, 1, 0), 0))
            return jnp.where(n < need, cand, prefix)
        return lax.fori_loop(0, idx_bits, index_bit, jnp.zeros((1, qb), I32))

    last = lax.cond(excess > 0, tie_search, lambda: jnp.full((1, qb), 2 ** idx_bits, I32))

    def bias_chunk(c, carry):
        ks = pl.multiple_of(c * kc, kc)
        key = keys_ref[pl.ds(ks, kc), :]
        idx = ks + row
        tied = jnp.where(key == thr, jnp.where(idx <= last, 0.0, NEG_BIG), NEG_BIG)
        bias = jnp.where(key > thr, 0.0, tied)
        bias_ref[pl.ds(ks, kc), :] = jnp.where(idx <= q_pos, bias, NEG_BIG)
        return carry

    lax.fori_loop(0, nk, bias_chunk, 0)

    rep = ATT_HEADS // ATT_KV_HEADS
    hd = ATT_HEAD_DIM
    wide = rep * qb
    groups = range(ATT_KV_HEADS)
    qgs = [jnp.concatenate([qt_ref[(g * rep + r) * hd:(g * rep + r + 1) * hd, :] for r in range(rep)], axis=1)
           for g in groups]

    def fold(x, op):
        return op(x.reshape(kc // SUBLANES, SUBLANES, wide), axis=0)

    def score_sweep(c, carry):
        ks = pl.multiple_of(c * kc, kc)
        b = bias_ref[pl.ds(ks, kc), :]
        b = jnp.concatenate([b] * rep, axis=1)
        out = []
        for g in groups:
            s = _dot(k_ref[pl.ds(ks, kc), g * hd:(g + 1) * hd], qgs[g]) + b
            s_ref[g, pl.ds(ks, kc), :] = s
            out.append(jnp.maximum(carry[g], fold(s, jnp.max)))
        return tuple(out)

    tops = lax.fori_loop(0, nk, score_sweep, tuple(jnp.full((SUBLANES, wide), NEG_BIG, F32) for _ in groups))
    tops = [jnp.max(m, axis=0, keepdims=True) for m in tops]
    acc_ref[...] = jnp.zeros_like(acc_ref)

    def value_sweep(c, carry):
        ks = pl.multiple_of(c * kc, kc)
        out = []
        for g in groups:
            p = jnp.exp(s_ref[g, pl.ds(ks, kc), :] - tops[g])
            acc_ref[g] += _dot(v_ref[c, g * hd:(g + 1) * hd, :], p.astype(BF16))
            out.append(carry[g] + fold(p, jnp.sum))
        return tuple(out)

    sums = lax.fori_loop(0, nk, value_sweep, tuple(jnp.zeros((SUBLANES, wide), F32) for _ in groups))
    for g in groups:
        o = acc_ref[g] / jnp.sum(sums[g], axis=0, keepdims=True)
        for r in range(rep):
            h = g * rep + r
            o_ref[:, h * hd:(h + 1) * hd] = o[:, r * qb:(r + 1) * qb].T.astype(o_ref.dtype)


def dsa_attention(ki, k, v3, qit, qt, wit, batch, seq):
    t = batch * seq
    nqb = seq // Q_BLOCK
    kvw = ATT_KV_HEADS * ATT_HEAD_DIM
    topk = min(TOPK_MAX, seq // 4)
    idx_bits = max(1, int(math.ceil(math.log2(seq))))
    assert seq % (2 * KEY_CHUNK) == 0 and KEY_CHUNK % Q_BLOCK == 0
    qcol = lambda r: pl.BlockSpec((r, Q_BLOCK), lambda b, j: (0, b * nqb + j))
    return pl.pallas_call(
        functools.partial(_dsa_kernel, topk=topk, idx_bits=idx_bits),
        grid=(batch, nqb),
        in_specs=[pl.BlockSpec((seq, IDX_DIM), lambda b, j: (b, 0)),
                  pl.BlockSpec((seq, kvw), lambda b, j: (b, 0)),
                  pl.BlockSpec((seq // KEY_CHUNK, kvw, KEY_CHUNK), lambda b, j: (b, 0, 0)),
                  pl.BlockSpec((1, IDX_DIM, IDX_HEADS * Q_BLOCK), lambda b, j: (b * nqb + j, 0, 0)),
                  qcol(ATT_WIDTH), qcol(IDX_HEADS)],
        out_specs=pl.BlockSpec((Q_BLOCK, ATT_WIDTH), lambda b, j: (b * nqb + j, 0)),
        out_shape=jax.ShapeDtypeStruct((t, ATT_WIDTH), BF16),
        scratch_shapes=[pltpu.VMEM((seq, Q_BLOCK), I32), pltpu.VMEM((seq, Q_BLOCK), F32),
                        pltpu.VMEM((ATT_KV_HEADS, ATT_HEAD_DIM, (ATT_HEADS // ATT_KV_HEADS) * Q_BLOCK), F32),
                        pltpu.VMEM((ATT_KV_HEADS, seq, (ATT_HEADS // ATT_KV_HEADS) * Q_BLOCK), F32),
                        pltpu.VMEM((seq, Q_BLOCK), I16), pltpu.VMEM((seq, Q_BLOCK), I16)],
        compiler_params=_params(("parallel", "arbitrary")),
        name="dsa_attention",
    )(ki, k, v3, qit, qt, wit)


def _merge_kernel(x_ref, ys_ref, ya_ref, gate_ref, wbs_ref, wba_ref, wo_ref, o_ref):
    d = x_ref.shape[1]
    gates = gate_ref[...]
    merged = (_sigmoid(gates[:, :d]) * _dot(ys_ref[...], wbs_ref[...])
              + _sigmoid(gates[:, d:]) * _dot(ya_ref[...], wba_ref[...]))
    o_ref[...] = x_ref[...] + _dot(merged.astype(BF16), wo_ref[...])


def merge_branches(x, y_ssm, y_att, gates, w_bs, w_ba, w_o, tm=512):
    t, d = x.shape
    tm = min(tm, t)
    tok = lambda w: pl.BlockSpec((tm, w), lambda i: (i, 0))
    const = lambda r, w: pl.BlockSpec((r, w), lambda i: (0, 0))
    return pl.pallas_call(
        _merge_kernel,
        grid=(t // tm,),
        in_specs=[tok(d), tok(D_INNER), tok(ATT_WIDTH), tok(2 * d),
                  const(D_INNER, d), const(ATT_WIDTH, d), const(d, d)],
        out_specs=tok(d),
        out_shape=jax.ShapeDtypeStruct((t, d), F32),
        compiler_params=_params(("parallel",)),
        name="merge_branches",
    )(x, y_ssm, y_att, gates, w_bs, w_ba, w_o)


def _top_rows(s, k, rows_out):
    rows = lax.broadcasted_iota(I32, (rows_out, s.shape[1]), 0)

    def body(r, carry):
        s, top = carry
        m = jnp.max(s, axis=0, keepdims=True)
        return jnp.where(s == m, -jnp.inf, s), jnp.where(rows == r, m, top)

    _, top = lax.fori_loop(0, k, body, (s, jnp.full((rows_out, s.shape[1]), -jnp.inf, F32)))
    return top


def _peer_route_kernel(x_ref, g_ref, wqt_ref, keys_ref, ht_ref, tau_ref, e1_ref, e2_ref):
    tm = x_ref.shape[0]
    h = _rms(x_ref[...], g_ref[...])
    ht_ref[...] = h.T.astype(BF16)
    qt = lax.dot_general(wqt_ref[...], h.astype(BF16), _NT, preferred_element_type=F32).astype(BF16)
    half = PEER_QDIM // 2
    k = PEER_TOPK
    k1 = k + 1
    rows_out = -(-k1 // SUBLANES) * SUBLANES
    sub8 = lax.broadcasted_iota(I32, (SUBLANES, tm), 0)
    for hh in range(PEER_HEADS):
        s1 = _dot(keys_ref[2 * hh], qt[(2 * hh) * half:(2 * hh + 1) * half, :])
        s2 = _dot(keys_ref[2 * hh + 1], qt[(2 * hh + 1) * half:(2 * hh + 2) * half, :])
        a = _top_rows(s1, k1, rows_out)
        b = _top_rows(s2, k1, rows_out)
        assert k1 // (SUBLANES + 1) <= 1
        tiles = [a[r:r + SUBLANES] + b[0:1] for r in range(SUBLANES, rows_out, SUBLANES)]
        for i in range(SUBLANES):
            lim = k1 // (i + 1)
            for r in range(0, lim, SUBLANES):
                tile = b[r:r + SUBLANES] + a[i:i + 1]
                tiles.append(tile if lim - r >= SUBLANES else jnp.where(sub8 < lim - r, tile, -jnp.inf))
        cand = jnp.concatenate(tiles, axis=0)
        best = a[0:1] + b[0:1]

        def body(r, carry):
            cand, zsum, kth, nxt = carry
            m = jnp.max(cand, axis=0, keepdims=True)
            zsum = zsum + jnp.where(r < k, jnp.exp(m - best), 0.0)
            return jnp.where(cand == m, -jnp.inf, cand), zsum, jnp.where(r == k - 1, m, kth), m

        _, zsum, kth, nxt = lax.fori_loop(0, k1, body, (cand, jnp.zeros((1, tm), F32), best, best))
        cut = 0.5 * (kth + nxt)
        r0 = hh * N_KEYS
        tau_ref[r0:r0 + N_KEYS, :] = jnp.exp((cut - b[0:1]) - s1)
        e1_ref[r0:r0 + N_KEYS, :] = jnp.exp(s1 - a[0:1]) / zsum
        e2_ref[r0:r0 + N_KEYS, :] = jnp.exp(s2 - b[0:1])


def peer_route(x, g, wq_t, keys, tm=256):
    t, d = x.shape
    tm = min(tm, t)
    rows = PEER_HEADS * N_KEYS
    col = lambda r: pl.BlockSpec((r, tm), lambda i: (0, i))
    return pl.pallas_call(
        _peer_route_kernel,
        grid=(t // tm,),
        in_specs=[pl.BlockSpec((tm, d), lambda i: (i, 0)),
                  pl.BlockSpec((1, d), lambda i: (0, 0)),
                  pl.BlockSpec(wq_t.shape, lambda i: (0, 0)),
                  pl.BlockSpec(keys.shape, lambda i: (0, 0, 0))],
        out_specs=[col(d), col(rows), col(rows), col(rows)],
        out_shape=[jax.ShapeDtypeStruct((d, t), BF16)] + [jax.ShapeDtypeStruct((rows, t), F32)] * 3,
        compiler_params=_params(("parallel",)),
        name="peer_route",
    )(x, g, wq_t, keys)


PEER_ROWS = 32


def _peer_dense_kernel(x_ref, ht_ref, tau_ref, e1_ref, e2_ref, u_ref, vt_ref, o_ref, acc_ref, hid_ref, act_ref,
                       *, ib, nblk):
    j = pl.program_id(1)
    tt = ht_ref.shape[1]
    cur = j % 2
    prev = 1 - cur

    @pl.when(j == 0)
    def _():
        acc_ref[...] = jnp.zeros_like(acc_ref)
        hid_ref[...] = jnp.zeros_like(hid_ref)
        act_ref[...] = jnp.zeros_like(act_ref)

    acc_ref[...] += _dot(vt_ref[...], act_ref[cur])
    blk = jnp.clip(j - 1, 0, nblk - 1)
    for ii in range(ib):
        i1 = blk * ib + ii
        taus = [tau_ref[pl.ds(hh * N_KEYS + i1, 1), :] for hh in range(PEER_HEADS)]
        e1s = [e1_ref[pl.ds(hh * N_KEYS + i1, 1), :] for hh in range(PEER_HEADS)]
        for lc in range(tt // LANES):
            cols = slice(lc * LANES, (lc + 1) * LANES)
            tau_b = [jnp.broadcast_to(taus[hh][:, cols], (PEER_ROWS, LANES)) for hh in range(PEER_HEADS)]
            e1_b = [jnp.broadcast_to(e1s[hh][:, cols], (PEER_ROWS, LANES)) for hh in range(PEER_HEADS)]
            for rb in range(N_KEYS // PEER_ROWS):
                gate = jnp.zeros((PEER_ROWS, LANES), F32)
                for hh in range(PEER_HEADS):
                    r0 = hh * N_KEYS + rb * PEER_ROWS
                    e2 = e2_ref[r0:r0 + PEER_ROWS, cols]
                    gate = gate + jnp.where(e2 >= tau_b[hh], e2, 0.0) * e1_b[hh]
                rows = slice(ii * N_KEYS + rb * PEER_ROWS, ii * N_KEYS + (rb + 1) * PEER_ROWS)
                hb = hid_ref[prev, rows, cols]
                act = 0.5 * hb * (1.0 + lax.erf(hb * np.float32(math.sqrt(0.5))))
                act_ref[prev, rows, cols] = (act * gate).astype(BF16)
    hid_ref[cur] = _dot(u_ref[...], ht_ref[...])

    @pl.when(j == pl.num_programs(1) - 1)
    def _():
        o_ref[...] = x_ref[...] + acc_ref[...].T


def peer_dense(x, h_t, tau, e1, e2, u, v_t, tt=512, ib=4):
    t, d = x.shape
    tt = min(tt, t)
    rows = PEER_HEADS * N_KEYS
    nblk = N_KEYS // ib
    col = lambda r: pl.BlockSpec((r, tt), lambda i, j: (0, i))
    tok = pl.BlockSpec((tt, d), lambda i, j: (i, 0))
    return pl.pallas_call(
        functools.partial(_peer_dense_kernel, ib=ib, nblk=nblk),
        grid=(t // tt, nblk + 2),
        in_specs=[tok, col(d), col(rows), col(rows), col(rows),
                  pl.BlockSpec((ib * N_KEYS, d), lambda i, j: (jnp.minimum(j, nblk - 1), 0)),
                  pl.BlockSpec((d, ib * N_KEYS), lambda i, j: (0, jnp.clip(j - 2, 0, nblk - 1)))],
        out_specs=tok,
        out_shape=jax.ShapeDtypeStruct((t, d), F32),
        scratch_shapes=[pltpu.VMEM((d, tt), F32), pltpu.VMEM((2, ib * N_KEYS, tt), F32),
                        pltpu.VMEM((2, ib * N_KEYS, tt), BF16)],
        compiler_params=_params(("parallel", "arbitrary")),
        name="peer_dense",
    )(x, h_t, tau, e1, e2, u, v_t)


def _ple_kernel(x_ref, p_ref, g_ref, wg_ref, wp_ref, o_ref):
    x = x_ref[...]
    h = _rms(x, g_ref[...]).astype(BF16)
    gate = _sigmoid(_dot(h, wg_ref[...]))
    o_ref[...] = x + gate * _dot(p_ref[...].astype(BF16), wp_ref[...])


def ple_block(x, p, g, w_gate, w_proj, tm=512):
    t, d = x.shape
    tm = min(tm, t)
    pd = p.shape[1]
    return pl.pallas_call(
        _ple_kernel,
        grid=(t // tm,),
        in_specs=[pl.BlockSpec((tm, d), lambda i: (i, 0)),
                  pl.BlockSpec((tm, pd), lambda i: (i, 0)),
                  pl.BlockSpec((1, d), lambda i: (0, 0)),
                  pl.BlockSpec((d, d), lambda i: (0, 0)),
                  pl.BlockSpec((pd, d), lambda i: (0, 0))],
        out_specs=pl.BlockSpec((tm, d), lambda i: (i, 0)),
        out_shape=jax.ShapeDtypeStruct((t, d), F32),
        compiler_params=_params(("parallel",)),
        name="ple_block",
    )(x, p, g, w_gate, w_proj)


def _split_w_in(w):
    sizes = [D_INNER, CONV_DIM, SSM_HEADS, ATT_WIDTH, ATT_KV_HEADS * ATT_HEAD_DIM,
             ATT_KV_HEADS * ATT_HEAD_DIM, IDX_HEADS * IDX_DIM, IDX_DIM, IDX_HEADS, 2 * w.shape[0]]
    out, acc = [], 0
    for s in sizes:
        out.append(w[:, acc:acc + s])
        acc += s
    return out


def kernel(x, p, positions, norm_mix, w_in, conv_w, conv_b, dt_bias, a_log, d_skip, ssm_norm, q_norm, k_norm,
           w_branch, w_out, norm_ffn, peer_wq, peer_keys, peer_u, peer_v, norm_ple, w_ple_gate, w_ple_proj):
    batch, seq, d = x.shape
    t = batch * seq
    depth = w_in.shape[0]
    tables = rope_tables_t(positions)
    xf = x.reshape(t, d)
    for i in range(depth):
        wz, wxbc, wdt, wq, wk, wv, wqi, wki, wwi, wg = _split_w_in(w_in[i])
        w_dt = jnp.pad(wdt, ((0, 0), (0, LANES - SSM_HEADS))).astype(BF16)
        w_att_t = jnp.transpose(jnp.concatenate([wq, wk, wv, wqi, wki, wwi], axis=1)).astype(BF16)
        g_mix = norm_mix[i].reshape(1, d)

        z = norm_matmul(xf, g_mix, wz.astype(BF16))
        xbc = norm_matmul(xf, g_mix, wxbc.astype(BF16))
        dt_slab = norm_matmul(xf, g_mix, w_dt)
        gates = norm_matmul(xf, g_mix, wg.astype(BF16))
        proj_t = norm_matmul_t(xf, g_mix, w_att_t)

        y_ssm = ssm_branch(z, xbc, dt_slab, conv_w[i], conv_b[i], dt_bias[i], a_log[i], d_skip[i],
                           ssm_norm[i], batch, seq)
        qt, k, v3, qit, ki, wit = dsa_prep(proj_t, tables, q_norm[i], k_norm[i])
        y_att = dsa_attention(ki, k, v3, qit, qt, wit, batch, seq)
        xf = merge_branches(xf, y_ssm, y_att, gates, w_branch[i, :D_INNER].astype(BF16),
                            w_branch[i, D_INNER:].astype(BF16), w_out[i].astype(BF16))

        keys = peer_keys[i].reshape(PEER_HEADS * 2, N_KEYS, PEER_QDIM // 2).astype(BF16)
        h_t, tau, e1, e2 = peer_route(xf, norm_ffn[i].reshape(1, d), jnp.transpose(peer_wq[i]).astype(BF16), keys)
        xf = peer_dense(xf, h_t, tau, e1, e2, peer_u[i].astype(BF16), jnp.transpose(peer_v[i]).astype(BF16))

        xf = ple_block(xf, p[i].reshape(t, -1), norm_ple[i].reshape(1, d),
                       w_ple_gate[i].astype(BF16), w_ple_proj[i].astype(BF16))
    return xf.reshape(batch, seq, d)
```

```python
import functools
import math

import numpy as np
import jax
import jax.numpy as jnp
from jax import lax
from jax.experimental import pallas as pl
from jax.experimental.pallas import tpu as pltpu

F32 = jnp.float32
BF16 = jnp.bfloat16
I32 = jnp.int32

EPS = 1e-6
ROPE_THETA = 10000.0

SSM_HEADS = 32
SSM_HEAD_DIM = 64
D_INNER = SSM_HEADS * SSM_HEAD_DIM
SSM_GROUPS = 4
D_STATE = 128
CONV_WIDTH = 4
CHUNK = 128
CONV_DIM = D_INNER + 2 * SSM_GROUPS * D_STATE
ATT_HEADS = 8
ATT_KV_HEADS = 2
ATT_HEAD_DIM = 128
ATT_WIDTH = ATT_HEADS * ATT_HEAD_DIM
IDX_HEADS = 8
IDX_DIM = 64
TOPK_MAX = 256
Q_BLOCK = 128
KEY_CHUNK = 256
KEY_BITS = 32
PEER_HEADS = 8
N_KEYS = 128
PEER_TOPK = 16
PEER_QDIM = 256

LANES = 128
SUBLANES = 8
VMEM_LIMIT = 56 * 1024 * 1024
INT_MIN = np.int32(-2 ** 31)
NEG_BIG = -1e30

_NT = (((1,), (1,)), ((), ()))


def _params(sem):
    return pltpu.CompilerParams(dimension_semantics=sem, vmem_limit_bytes=VMEM_LIMIT)


def _rms(x, g):
    return x * lax.rsqrt(jnp.mean(x * x, axis=-1, keepdims=True) + EPS) * g


def _sigmoid(x):
    return 1.0 / (1.0 + jnp.exp(-x))


def _split3(a):
    a1 = a.astype(BF16)
    r1 = a - a1.astype(F32)
    a2 = r1.astype(BF16)
    a3 = (r1 - a2.astype(F32)).astype(BF16)
    return a1, a2, a3


def _dot(a, b):
    return jnp.dot(a, b, preferred_element_type=F32)


def _norm_mm_kernel(x_ref, g_ref, w_ref, o_ref):
    h = _rms(x_ref[...], g_ref[...]).astype(BF16)
    o_ref[...] = _dot(h, w_ref[...]).astype(o_ref.dtype)


def norm_matmul(x, g, w, out_dtype=F32, tm=512):
    m, k = x.shape
    n = w.shape[1]
    return pl.pallas_call(
        _norm_mm_kernel,
        grid=(m // tm,),
        in_specs=[pl.BlockSpec((tm, k), lambda i: (i, 0)),
                  pl.BlockSpec((1, k), lambda i: (0, 0)),
                  pl.BlockSpec((k, n), lambda i: (0, 0))],
        out_specs=pl.BlockSpec((tm, n), lambda i: (i, 0)),
        out_shape=jax.ShapeDtypeStruct((m, n), out_dtype),
        compiler_params=_params(("parallel",)),
        name="norm_matmul",
    )(x, g, w)


def _norm_mm_t_kernel(x_ref, g_ref, wt_ref, o_ref):
    h = _rms(x_ref[...], g_ref[...]).astype(BF16)
    o_ref[...] = lax.dot_general(wt_ref[...], h, _NT, preferred_element_type=F32)


def norm_matmul_t(x, g, wt, tm=512):
    m, k = x.shape
    n = wt.shape[0]
    return pl.pallas_call(
        _norm_mm_t_kernel,
        grid=(m // tm,),
        in_specs=[pl.BlockSpec((tm, k), lambda i: (i, 0)),
                  pl.BlockSpec((1, k), lambda i: (0, 0)),
                  pl.BlockSpec((n, k), lambda i: (0, 0))],
        out_specs=pl.BlockSpec((n, tm), lambda i: (0, i)),
        out_shape=jax.ShapeDtypeStruct((n, m), F32),
        compiler_params=_params(("parallel",)),
        name="norm_matmul_t",
    )(x, g, wt)


def _ssm_kernel(z_ref, xbc_ref, dt_ref, convw_ref, convb_ref, dtb_ref, alog_ref,
                dskip_ref, normg_ref, expand_ref, y_ref, carry_ref, state_ref):
    @pl.when(pl.program_id(1) == 0)
    def _():
        carry_ref[...] = jnp.zeros_like(carry_ref)
        state_ref[...] = jnp.zeros_like(state_ref)

    xbc = xbc_ref[...]
    ext = jnp.concatenate([carry_ref[...], xbc], axis=0)
    conv = convb_ref[...]
    for j in range(CONV_WIDTH):
        lo = SUBLANES - (CONV_WIDTH - 1) + j
        conv = conv + convw_ref[j:j + 1, :] * ext[lo:lo + CHUNK]
    carry_ref[...] = xbc[CHUNK - SUBLANES:]
    xc = conv * _sigmoid(conv)
    xs = xc[:, :D_INNER]
    gw = SSM_GROUPS * D_STATE
    bm = xc[:, D_INNER:D_INNER + gw]
    cm = xc[:, D_INNER + gw:]

    lane = lax.broadcasted_iota(I32, (CHUNK, LANES), 1)
    row = lax.broadcasted_iota(I32, (CHUNK, LANES), 0)
    head_lane = lane < SSM_HEADS
    dt = jax.nn.softplus(dt_ref[...] + dtb_ref[...])
    adt = jnp.where(head_lane, dt * (-jnp.exp(alog_ref[...])), 0.0)
    tril = jnp.where(row >= lane, 1.0, 0.0).astype(BF16)
    cs = sum(_dot(tril, part) for part in _split3(adt))
    cs_row = cs.T
    cs_last = cs[CHUNK - 1:CHUNK, :]
    ecs = jnp.exp(cs)
    decay = jnp.exp(cs_last - cs)

    expand = expand_ref[...]

    def widen(a):
        return sum(_dot(part, expand) for part in _split3(a))

    dt_w = widen(dt)
    ecs_w = widen(ecs)
    decay_w = widen(decay)
    xdt = xs * dt_w
    xdt_b = xdt.astype(BF16)
    xdec_b = (xdt * decay_w).astype(BF16)
    causal = row >= lane
    pair_lo = lane < SSM_HEAD_DIM

    gh = SSM_HEADS // SSM_GROUPS
    gcols = gh * SSM_HEAD_DIM
    ys = []
    for g in range(SSM_GROUPS):
        bg = bm[:, g * D_STATE:(g + 1) * D_STATE]
        cg = cm[:, g * D_STATE:(g + 1) * D_STATE].astype(BF16)
        cb = lax.dot_general(cg, bg.astype(BF16), _NT, preferred_element_type=F32)
        st = state_ref[g]
        y_off = _dot(cg, st.astype(BF16))
        new_st = _dot(bg.T.astype(BF16), xdec_b[:, g * gcols:(g + 1) * gcols])
        state_ref[g] = st * ecs_w[CHUNK - 1:CHUNK, g * gcols:(g + 1) * gcols] + new_st
        pieces = []
        for pr in range(gh // 2):
            outs = []
            for sub in range(2):
                h = g * gh + pr * 2 + sub
                lm = jnp.where(causal, jnp.exp(cs[:, h:h + 1] - cs_row[h:h + 1, :]), 0.0)
                mm = (cb * lm).astype(BF16)
                c0 = (h // 2) * LANES
                outs.append(_dot(mm, xdt_b[:, c0:c0 + LANES]))
            pieces.append(jnp.where(pair_lo, outs[0], outs[1]))
        y_diag = jnp.concatenate(pieces, axis=1)
        ys.append(y_diag + y_off * ecs_w[:, g * gcols:(g + 1) * gcols])
    y = jnp.concatenate(ys, axis=1) + dskip_ref[...] * xs
    z = z_ref[...]
    y = y * (z * _sigmoid(z))
    outs = []
    for g in range(SSM_GROUPS):
        yg = y[:, g * gcols:(g + 1) * gcols]
        outs.append(yg * lax.rsqrt(jnp.mean(yg * yg, axis=-1, keepdims=True) + EPS))
    y_ref[...] = (jnp.concatenate(outs, axis=1) * normg_ref[...]).astype(y_ref.dtype)


def ssm_branch(z, xbc, dt_slab, conv_w, conv_b, dt_bias, a_log, d_skip, norm_g, batch, seq):
    t = batch * seq
    nchunk = seq // CHUNK
    pad = LANES - SSM_HEADS
    convw_t = jnp.transpose(conv_w)
    dtb = jnp.pad(dt_bias, (0, pad)).reshape(1, LANES)
    alog = jnp.pad(a_log, (0, pad)).reshape(1, LANES)
    dskip = jnp.repeat(d_skip, SSM_HEAD_DIM).reshape(1, D_INNER)
    expand = (np.arange(LANES)[:, None] == (np.arange(D_INNER)[None, :] // SSM_HEAD_DIM))
    expand = jnp.asarray(expand, dtype=BF16)
    tok = lambda w: pl.BlockSpec((CHUNK, w), lambda b, c: (b * nchunk + c, 0))
    const = lambda r, w: pl.BlockSpec((r, w), lambda b, c: (0, 0))
    return pl.pallas_call(
        _ssm_kernel,
        grid=(batch, nchunk),
        in_specs=[tok(D_INNER), tok(CONV_DIM), tok(LANES),
                  const(CONV_WIDTH, CONV_DIM), const(1, CONV_DIM), const(1, LANES),
                  const(1, LANES), const(1, D_INNER), const(1, D_INNER),
                  const(LANES, D_INNER)],
        out_specs=tok(D_INNER),
        out_shape=jax.ShapeDtypeStruct((t, D_INNER), BF16),
        scratch_shapes=[pltpu.VMEM((SUBLANES, CONV_DIM), F32),
                        pltpu.VMEM((SSM_GROUPS, D_STATE, D_INNER // SSM_GROUPS), F32)],
        compiler_params=_params(("parallel", "arbitrary")),
        name="ssm_branch",
    )(z, xbc, dt_slab, convw_t, conv_b.reshape(1, CONV_DIM), dtb, alog, dskip,
      norm_g.reshape(1, D_INNER), expand)


def _rope_kernel(pos_ref, inva_ref, invi_ref, ca_ref, sa_ref, ci_ref, si_ref):
    pos = pos_ref[...]
    ang_a = inva_ref[...] * pos
    ang_i = invi_ref[...] * pos
    ca_ref[...] = jnp.cos(ang_a)
    sa_ref[...] = jnp.sin(ang_a)
    ci_ref[...] = jnp.cos(ang_i)
    si_ref[...] = jnp.sin(ang_i)


def rope_tables_t(positions, tile=2048):
    t = positions.size
    tile = min(tile, t)
    pos = positions.reshape(1, t).astype(F32)
    ha, hi = ATT_HEAD_DIM // 2, IDX_DIM // 2
    inv_a = (1.0 / (ROPE_THETA ** (jnp.arange(0, ATT_HEAD_DIM, 2, dtype=F32) / ATT_HEAD_DIM))).reshape(ha, 1)
    inv_i = (1.0 / (ROPE_THETA ** (jnp.arange(0, IDX_DIM, 2, dtype=F32) / IDX_DIM))).reshape(hi, 1)
    col = lambda r: pl.BlockSpec((r, tile), lambda i: (0, i))
    return pl.pallas_call(
        _rope_kernel,
        grid=(t // tile,),
        in_specs=[col(1), pl.BlockSpec((ha, 1), lambda i: (0, 0)), pl.BlockSpec((hi, 1), lambda i: (0, 0))],
        out_specs=[col(ha), col(ha), col(hi), col(hi)],
        out_shape=[jax.ShapeDtypeStruct((ha, t), F32)] * 2 + [jax.ShapeDtypeStruct((hi, t), F32)] * 2,
        compiler_params=_params(("parallel",)),
        name="rope_tables",
    )(pos, inv_a, inv_i)


_Q0 = 0
_K0 = _Q0 + ATT_WIDTH
_V0 = _K0 + ATT_KV_HEADS * ATT_HEAD_DIM
_QI0 = _V0 + ATT_KV_HEADS * ATT_HEAD_DIM
_KI0 = _QI0 + IDX_HEADS * IDX_DIM
_WI0 = _KI0 + IDX_DIM
_ATT_ROWS = _WI0 + IDX_HEADS


def _rope_rows(x, cos, sin):
    half = x.shape[0] // 2
    x1, x2 = x[:half], x[half:]
    return jnp.concatenate([x1 * cos - x2 * sin, x1 * sin + x2 * cos], axis=0)


def _head_norm_rows(x, g):
    return x * lax.rsqrt(jnp.mean(x * x, axis=0, keepdims=True) + EPS) * g


def _dsa_prep_kernel(p_ref, ca_ref, sa_ref, ci_ref, si_ref, qg_ref, kg_ref,
                     qt_ref, k_ref, v_ref, qit_ref, ki_ref, wit_ref):
    tile = p_ref.shape[1]
    ca, sa, ci, si = ca_ref[...], sa_ref[...], ci_ref[...], si_ref[...]
    hd = ATT_HEAD_DIM
    for h in range(ATT_HEADS):
        x = _head_norm_rows(p_ref[_Q0 + h * hd:_Q0 + (h + 1) * hd, :], qg_ref[...])
        qt_ref[h * hd:(h + 1) * hd, :] = (_rope_rows(x, ca, sa) * (hd ** -0.5)).astype(BF16)
    for h in range(ATT_KV_HEADS):
        x = _head_norm_rows(p_ref[_K0 + h * hd:_K0 + (h + 1) * hd, :], kg_ref[...])
        k_ref[:, h * hd:(h + 1) * hd] = _rope_rows(x, ca, sa).T.astype(BF16)
    for s in range(tile // KEY_CHUNK):
        v_ref[s] = p_ref[_V0:_V0 + ATT_KV_HEADS * hd, s * KEY_CHUNK:(s + 1) * KEY_CHUNK].astype(BF16)
    for h in range(IDX_HEADS):
        x = _rope_rows(p_ref[_QI0 + h * IDX_DIM:_QI0 + (h + 1) * IDX_DIM, :], ci, si) * (IDX_DIM ** -0.5)
        x = x.astype(BF16)
        for s in range(tile // Q_BLOCK):
            qit_ref[s, :, h * Q_BLOCK:(h + 1) * Q_BLOCK] = x[:, s * Q_BLOCK:(s + 1) * Q_BLOCK]
    x = _rope_rows(p_ref[_KI0:_KI0 + IDX_DIM, :], ci, si)
    zpad = jnp.zeros((LANES - IDX_DIM, tile), F32)
    ki_ref[...] = jnp.concatenate([x, zpad], axis=0).T[:, :IDX_DIM].astype(BF16)
    wit_ref[...] = p_ref[_WI0:_WI0 + IDX_HEADS, :] * (IDX_HEADS ** -0.5)


def dsa_prep(proj_t, tables, q_norm, k_norm, tile=512):
    t = proj_t.shape[1]
    tile = min(tile, t)
    ca, sa, ci, si = tables
    ha, hi = ATT_HEAD_DIM // 2, IDX_DIM // 2
    col = lambda r: pl.BlockSpec((r, tile), lambda i: (0, i))
    rowb = lambda w: pl.BlockSpec((tile, w), lambda i: (i, 0))
    const = lambda r, w: pl.BlockSpec((r, w), lambda i: (0, 0))
    kvw = ATT_KV_HEADS * ATT_HEAD_DIM
    return pl.pallas_call(
        _dsa_prep_kernel,
        grid=(t // tile,),
        in_specs=[col(_ATT_ROWS), col(ha), col(ha), col(hi), col(hi),
                  const(ATT_HEAD_DIM, 1), const(ATT_HEAD_DIM, 1)],
        out_specs=[col(ATT_WIDTH), rowb(kvw),
                   pl.BlockSpec((tile // KEY_CHUNK, kvw, KEY_CHUNK), lambda i: (i, 0, 0)),
                   pl.BlockSpec((tile // Q_BLOCK, IDX_DIM, IDX_HEADS * Q_BLOCK), lambda i: (i, 0, 0)),
                   rowb(IDX_DIM), col(IDX_HEADS)],
        out_shape=[jax.ShapeDtypeStruct((ATT_WIDTH, t), BF16),
                   jax.ShapeDtypeStruct((t, kvw), BF16),
                   jax.ShapeDtypeStruct((t // KEY_CHUNK, kvw, KEY_CHUNK), BF16),
                   jax.ShapeDtypeStruct((t // Q_BLOCK, IDX_DIM, IDX_HEADS * Q_BLOCK), BF16),
                   jax.ShapeDtypeStruct((t, IDX_DIM), BF16),
                   jax.ShapeDtypeStruct((IDX_HEADS, t), F32)],
        compiler_params=_params(("parallel",)),
        name="dsa_prep",
    )(proj_t, ca, sa, ci, si, q_norm.reshape(ATT_HEAD_DIM, 1), k_norm.reshape(ATT_HEAD_DIM, 1))


def _bit_transpose32(words):
    a = list(words)
    for j, mask in ((16, 0x0000FFFF), (8, 0x00FF00FF), (4, 0x0F0F0F0F), (2, 0x33333333), (1, 0x55555555)):
        for k in range(32):
            if k & j == 0:
                t = (lax.shift_right_logical(a[k], np.int32(j)) ^ a[k + j]) & np.int32(mask)
                a[k + j] = a[k + j] ^ t
                a[k] = a[k] ^ lax.shift_left(t, np.int32(j))
    return a


def _dsa_kernel(ki_ref, k_ref, v_ref, qit_ref, qt_ref, wit_ref, o_ref, keys_ref, bias_ref, acc_ref, s_ref,
                planes_ref, *, topk, idx_bits):
    qb = Q_BLOCK
    kc = KEY_CHUNK
    j = pl.program_id(1)
    nk = (j * qb + qb + kc - 1) // kc
    row = lax.broadcasted_iota(I32, (kc, qb), 0)
    col = lax.broadcasted_iota(I32, (kc, qb), 1)
    q_pos = j * qb + col

    def score_chunk(c, carry):
        ks = pl.multiple_of(c * kc, kc)
        s = _dot(ki_ref[pl.ds(ks, kc), :], qit_ref[0])
        acc = jnp.zeros((kc, qb), F32)
        for h in range(IDX_HEADS):
            acc = acc + wit_ref[h:h + 1, :] * jnp.maximum(s[:, h * qb:(h + 1) * qb], 0.0)
        acc = jnp.where(acc == 0.0, 0.0, acc)
        bits = pltpu.bitcast(acc, I32)
        key = jnp.where(bits < 0, bits ^ np.int32(0x7FFFFFFF), bits)
        key = jnp.where(ks + row <= q_pos, key, INT_MIN)
        keys_ref[pl.ds(ks, kc), :] = key
        u = key ^ INT_MIN
        planes = _bit_transpose32([u[i * SUBLANES:(i + 1) * SUBLANES] for i in range(KEY_BITS)])
        ps = pl.multiple_of(c * SUBLANES, SUBLANES)
        for b in range(KEY_BITS):
            planes_ref[b, pl.ds(ps, SUBLANES), :] = planes[b]
        return carry

    lax.fori_loop(0, nk, score_chunk, 0)

    def clear_chunk(c, carry):
        ps = pl.multiple_of(c * SUBLANES, SUBLANES)
        for b in range(KEY_BITS):
            planes_ref[b, pl.ds(ps, SUBLANES), :] = jnp.zeros((SUBLANES, qb), I32)
        return carry

    lax.fori_loop(nk, planes_ref.shape[1] // SUBLANES, clear_chunk, 0)

    @pl.when(nk % 2 == 1)
    def _():
        keys_ref[pl.ds(pl.multiple_of(nk * kc, kc), kc), :] = jnp.full((kc, qb), INT_MIN, I32)

    kc2 = 2 * kc
    row2 = lax.broadcasted_iota(I32, (kc2, qb), 0)

    def count(hit):
        def body(c, acc):
            ks = pl.multiple_of(c * kc2, kc2)
            one = hit(keys_ref[pl.ds(ks, kc2), :], ks + row2)
            return acc + one.reshape(kc2 // SUBLANES, SUBLANES, qb).sum(axis=0)
        acc = lax.fori_loop(0, (nk + 1) // 2, body, jnp.zeros((SUBLANES, qb), I32))
        return acc.sum(axis=0, keepdims=True)

    nrows = planes_ref.shape[1]
    prow = lax.broadcasted_iota(I32, (nrows, qb), 0)
    active0 = jnp.where(prow < nk * SUBLANES, np.int32(-1), np.int32(0))

    def value_bit(i, carry):
        active, need, prefix = carry
        hit = active & planes_ref[KEY_BITS - 1 - i]
        n = jnp.sum(lax.population_count(hit), axis=0, keepdims=True)
        take = n >= need
        active = jnp.where(take, hit, active ^ hit)
        need = jnp.where(take, need, need - n)
        prefix = jnp.where(take, prefix | jnp.left_shift(np.int32(1), KEY_BITS - 1 - i), prefix)
        return active, need, prefix

    active, need, prefix = lax.fori_loop(
        0, KEY_BITS, value_bit, (active0, jnp.full((1, qb), topk, I32), jnp.zeros((1, qb), I32)))
    thr = prefix ^ INT_MIN

    n_tied = jnp.sum(lax.population_count(active), axis=0, keepdims=True)
    excess = jnp.max(jnp.where(thr > INT_MIN, n_tied - need, 0))

    def tie_search():
        def index_bit(i, prefix):
            cand = prefix | jnp.left_shift(np.int32(1), idx_bits - 1 - i)
            n = count(lambda key, idx: jnp.where(key == thr, jnp.where(idx < cand, 1, 0), 0))
            return jnp.where(n < need, cand, prefix)
        return lax.fori_loop(0, idx_bits, index_bit, jnp.zeros((1, qb), I32))

    last = lax.cond(excess > 0, tie_search, lambda: jnp.full((1, qb), 2 ** idx_bits, I32))

    def bias_chunk(c, carry):
        ks = pl.multiple_of(c * kc, kc)
        key = keys_ref[pl.ds(ks, kc), :]
        idx = ks + row
        tied = jnp.where(key == thr, jnp.where(idx <= last, 0.0, NEG_BIG), NEG_BIG)
        bias = jnp.where(key > thr, 0.0, tied)
        bias_ref[pl.ds(ks, kc), :] = jnp.where(idx <= q_pos, bias, NEG_BIG)
        return carry

    lax.fori_loop(0, nk, bias_chunk, 0)

    rep = ATT_HEADS // ATT_KV_HEADS
    hd = ATT_HEAD_DIM
    wide = rep * qb
    groups = range(ATT_KV_HEADS)
    qgs = [jnp.concatenate([qt_ref[(g * rep + r) * hd:(g * rep + r + 1) * hd, :] for r in range(rep)], axis=1)
           for g in groups]

    def fold(x, op):
        return op(x.reshape(kc // SUBLANES, SUBLANES, wide), axis=0)

    def score_sweep(c, carry):
        ks = pl.multiple_of(c * kc, kc)
        b = bias_ref[pl.ds(ks, kc), :]
        b = jnp.concatenate([b] * rep, axis=1)
        out = []
        for g in groups:
            s = _dot(k_ref[pl.ds(ks, kc), g * hd:(g + 1) * hd], qgs[g]) + b
            s_ref[g, pl.ds(ks, kc), :] = s
            out.append(jnp.maximum(carry[g], fold(s, jnp.max)))
        return tuple(out)

    tops = lax.fori_loop(0, nk, score_sweep, tuple(jnp.full((SUBLANES, wide), NEG_BIG, F32) for _ in groups))
    tops = [jnp.max(m, axis=0, keepdims=True) for m in tops]
    acc_ref[...] = jnp.zeros_like(acc_ref)

    def value_sweep(c, carry):
        ks = pl.multiple_of(c * kc, kc)
        out = []
        for g in groups:
            p = jnp.exp(s_ref[g, pl.ds(ks, kc), :] - tops[g])
            acc_ref[g] += _dot(v_ref[c, g * hd:(g + 1) * hd, :], p.astype(BF16))
            out.append(carry[g] + fold(p, jnp.sum))
        return tuple(out)

    sums = lax.fori_loop(0, nk, value_sweep, tuple(jnp.zeros((SUBLANES, wide), F32) for _ in groups))
    for g in groups:
        o = acc_ref[g] / jnp.sum(sums[g], axis=0, keepdims=True)
        for r in range(rep):
            h = g * rep + r
            o_ref[:, h * hd:(h + 1) * hd] = o[:, r * qb:(r + 1) * qb].T.astype(o_ref.dtype)


def dsa_attention(ki, k, v3, qit, qt, wit, batch, seq):
    t = batch * seq
    nqb = seq // Q_BLOCK
    kvw = ATT_KV_HEADS * ATT_HEAD_DIM
    topk = min(TOPK_MAX, seq // 4)
    idx_bits = max(1, int(math.ceil(math.log2(seq))))
    assert seq % (2 * KEY_CHUNK) == 0 and KEY_CHUNK % Q_BLOCK == 0
    qcol = lambda r: pl.BlockSpec((r, Q_BLOCK), lambda b, j: (0, b * nqb + j))
    return pl.pallas_call(
        functools.partial(_dsa_kernel, topk=topk, idx_bits=idx_bits),
        grid=(batch, nqb),
        in_specs=[pl.BlockSpec((seq, IDX_DIM), lambda b, j: (b, 0)),
                  pl.BlockSpec((seq, kvw), lambda b, j: (b, 0)),
                  pl.BlockSpec((seq // KEY_CHUNK, kvw, KEY_CHUNK), lambda b, j: (b, 0, 0)),
                  pl.BlockSpec((1, IDX_DIM, IDX_HEADS * Q_BLOCK), lambda b, j: (b * nqb + j, 0, 0)),
                  qcol(ATT_WIDTH), qcol(IDX_HEADS)],
        out_specs=pl.BlockSpec((Q_BLOCK, ATT_WIDTH), lambda b, j: (b * nqb + j, 0)),
        out_shape=jax.ShapeDtypeStruct((t, ATT_WIDTH), BF16),
        scratch_shapes=[pltpu.VMEM((seq, Q_BLOCK), I32), pltpu.VMEM((seq, Q_BLOCK), F32),
                        pltpu.VMEM((ATT_KV_HEADS, ATT_HEAD_DIM, (ATT_HEADS // ATT_KV_HEADS) * Q_BLOCK), F32),
                        pltpu.VMEM((ATT_KV_HEADS, seq, (ATT_HEADS // ATT_KV_HEADS) * Q_BLOCK), F32),
                        pltpu.VMEM((KEY_BITS, seq // KEY_BITS, Q_BLOCK), I32)],
        compiler_params=_params(("parallel", "arbitrary")),
        name="dsa_attention",
    )(ki, k, v3, qit, qt, wit)


def _merge_kernel(x_ref, ys_ref, ya_ref, gate_ref, wbs_ref, wba_ref, wo_ref, o_ref):
    d = x_ref.shape[1]
    gates = gate_ref[...]
    merged = (_sigmoid(gates[:, :d]) * _dot(ys_ref[...], wbs_ref[...])
              + _sigmoid(gates[:, d:]) * _dot(ya_ref[...], wba_ref[...]))
    o_ref[...] = x_ref[...] + _dot(merged.astype(BF16), wo_ref[...])


def merge_branches(x, y_ssm, y_att, gates, w_bs, w_ba, w_o, tm=512):
    t, d = x.shape
    tm = min(tm, t)
    tok = lambda w: pl.BlockSpec((tm, w), lambda i: (i, 0))
    const = lambda r, w: pl.BlockSpec((r, w), lambda i: (0, 0))
    return pl.pallas_call(
        _merge_kernel,
        grid=(t // tm,),
        in_specs=[tok(d), tok(D_INNER), tok(ATT_WIDTH), tok(2 * d),
                  const(D_INNER, d), const(ATT_WIDTH, d), const(d, d)],
        out_specs=tok(d),
        out_shape=jax.ShapeDtypeStruct((t, d), F32),
        compiler_params=_params(("parallel",)),
        name="merge_branches",
    )(x, y_ssm, y_att, gates, w_bs, w_ba, w_o)


def _top_rows(s, k, rows_out):
    rows = lax.broadcasted_iota(I32, (rows_out, s.shape[1]), 0)

    def body(r, carry):
        s, top = carry
        m = jnp.max(s, axis=0, keepdims=True)
        return jnp.where(s == m, -jnp.inf, s), jnp.where(rows == r, m, top)

    _, top = lax.fori_loop(0, k, body, (s, jnp.full((rows_out, s.shape[1]), -jnp.inf, F32)))
    return top


def _peer_route_kernel(x_ref, g_ref, wqt_ref, keys_ref, ht_ref, tau_ref, e1_ref, e2_ref):
    tm = x_ref.shape[0]
    h = _rms(x_ref[...], g_ref[...])
    ht_ref[...] = h.T.astype(BF16)
    qt = lax.dot_general(wqt_ref[...], h.astype(BF16), _NT, preferred_element_type=F32).astype(BF16)
    half = PEER_QDIM // 2
    k = PEER_TOPK
    k1 = k + 1
    rows_out = -(-k1 // SUBLANES) * SUBLANES
    sub8 = lax.broadcasted_iota(I32, (SUBLANES, tm), 0)
    for hh in range(PEER_HEADS):
        s1 = _dot(keys_ref[2 * hh], qt[(2 * hh) * half:(2 * hh + 1) * half, :])
        s2 = _dot(keys_ref[2 * hh + 1], qt[(2 * hh + 1) * half:(2 * hh + 2) * half, :])
        a = _top_rows(s1, k1, rows_out)
        b = _top_rows(s2, k1, rows_out)
        assert k1 // (SUBLANES + 1) <= 1
        tiles = [a[r:r + SUBLANES] + b[0:1] for r in range(SUBLANES, rows_out, SUBLANES)]
        for i in range(SUBLANES):
            lim = k1 // (i + 1)
            for r in range(0, lim, SUBLANES):
                tile = b[r:r + SUBLANES] + a[i:i + 1]
                tiles.append(tile if lim - r >= SUBLANES else jnp.where(sub8 < lim - r, tile, -jnp.inf))
        cand = jnp.concatenate(tiles, axis=0)
        best = a[0:1] + b[0:1]

        def body(r, carry):
            cand, zsum, kth, nxt = carry
            m = jnp.max(cand, axis=0, keepdims=True)
            zsum = zsum + jnp.where(r < k, jnp.exp(m - best), 0.0)
            return jnp.where(cand == m, -jnp.inf, cand), zsum, jnp.where(r == k - 1, m, kth), m

        _, zsum, kth, nxt = lax.fori_loop(0, k1, body, (cand, jnp.zeros((1, tm), F32), best, best))
        cut = 0.5 * (kth + nxt)
        r0 = hh * N_KEYS
        tau_ref[r0:r0 + N_KEYS, :] = jnp.exp((cut - b[0:1]) - s1)
        e1_ref[r0:r0 + N_KEYS, :] = jnp.exp(s1 - a[0:1]) * (0.5 / zsum)
        e2_ref[r0:r0 + N_KEYS, :] = jnp.exp(s2 - b[0:1])


def peer_route(x, g, wq_t, keys, tm=256):
    t, d = x.shape
    tm = min(tm, t)
    rows = PEER_HEADS * N_KEYS
    col = lambda r: pl.BlockSpec((r, tm), lambda i: (0, i))
    return pl.pallas_call(
        _peer_route_kernel,
        grid=(t // tm,),
        in_specs=[pl.BlockSpec((tm, d), lambda i: (i, 0)),
                  pl.BlockSpec((1, d), lambda i: (0, 0)),
                  pl.BlockSpec(wq_t.shape, lambda i: (0, 0)),
                  pl.BlockSpec(keys.shape, lambda i: (0, 0, 0))],
        out_specs=[col(d), col(rows), col(rows), col(rows)],
        out_shape=[jax.ShapeDtypeStruct((d, t), BF16)] + [jax.ShapeDtypeStruct((rows, t), F32)] * 3,
        compiler_params=_params(("parallel",)),
        name="peer_route",
    )(x, g, wq_t, keys)


PEER_ROWS = 32


def _peer_dense_kernel(x_ref, ht_ref, tau_ref, e1_ref, e2_ref, u_ref, vt_ref, o_ref, acc_ref, hid_ref, act_ref,
                       *, ib, nblk):
    j = pl.program_id(1)
    tt = ht_ref.shape[1]
    cur = j % 2
    prev = 1 - cur

    @pl.when(j == 0)
    def _():
        acc_ref[...] = jnp.zeros_like(acc_ref)
        hid_ref[...] = jnp.zeros_like(hid_ref)
        act_ref[...] = jnp.zeros_like(act_ref)

    acc_ref[...] += _dot(vt_ref[...], act_ref[cur])
    blk = jnp.clip(j - 1, 0, nblk - 1)
    for ii in range(ib):
        i1 = blk * ib + ii
        taus = [tau_ref[pl.ds(hh * N_KEYS + i1, 1), :] for hh in range(PEER_HEADS)]
        e1s = [e1_ref[pl.ds(hh * N_KEYS + i1, 1), :] for hh in range(PEER_HEADS)]
        for lc in range(tt // LANES):
            cols = slice(lc * LANES, (lc + 1) * LANES)
            tau_b = [jnp.broadcast_to(taus[hh][:, cols], (PEER_ROWS, LANES)) for hh in range(PEER_HEADS)]
            e1_b = [jnp.broadcast_to(e1s[hh][:, cols], (PEER_ROWS, LANES)) for hh in range(PEER_HEADS)]
            for rb in range(N_KEYS // PEER_ROWS):
                gate = jnp.zeros((PEER_ROWS, LANES), F32)
                for hh in range(PEER_HEADS):
                    r0 = hh * N_KEYS + rb * PEER_ROWS
                    e2 = e2_ref[r0:r0 + PEER_ROWS, cols]
                    gate = gate + jnp.where(e2 >= tau_b[hh], e2, 0.0) * e1_b[hh]
                rows = slice(ii * N_KEYS + rb * PEER_ROWS, ii * N_KEYS + (rb + 1) * PEER_ROWS)
                hb = hid_ref[prev, rows, cols]
                act = hb * (1.0 + lax.erf(hb * np.float32(math.sqrt(0.5))))
                act_ref[prev, rows, cols] = (act * gate).astype(BF16)
    hid_ref[cur] = _dot(u_ref[...], ht_ref[...])

    @pl.when(j == pl.num_programs(1) - 1)
    def _():
        o_ref[...] = x_ref[...] + acc_ref[...].T


def peer_dense(x, h_t, tau, e1, e2, u, v_t, tt=512, ib=4):
    t, d = x.shape
    tt = min(tt, t)
    rows = PEER_HEADS * N_KEYS
    nblk = N_KEYS // ib
    col = lambda r: pl.BlockSpec((r, tt), lambda i, j: (0, i))
    tok = pl.BlockSpec((tt, d), lambda i, j: (i, 0))
    return pl.pallas_call(
        functools.partial(_peer_dense_kernel, ib=ib, nblk=nblk),
        grid=(t // tt, nblk + 2),
        in_specs=[tok, col(d), col(rows), col(rows), col(rows),
                  pl.BlockSpec((ib * N_KEYS, d), lambda i, j: (jnp.minimum(j, nblk - 1), 0)),
                  pl.BlockSpec((d, ib * N_KEYS), lambda i, j: (0, jnp.clip(j - 2, 0, nblk - 1)))],
        out_specs=tok,
        out_shape=jax.ShapeDtypeStruct((t, d), F32),
        scratch_shapes=[pltpu.VMEM((d, tt), F32), pltpu.VMEM((2, ib * N_KEYS, tt), F32),
                        pltpu.VMEM((2, ib * N_KEYS, tt), BF16)],
        compiler_params=_params(("parallel", "arbitrary")),
        name="peer_dense",
    )(x, h_t, tau, e1, e2, u, v_t)


def _ple_kernel(x_ref, p_ref, g_ref, wg_ref, wp_ref, o_ref):
    x = x_ref[...]
    h = _rms(x, g_ref[...]).astype(BF16)
    gate = _sigmoid(_dot(h, wg_ref[...]))
    o_ref[...] = x + gate * _dot(p_ref[...].astype(BF16), wp_ref[...])


def ple_block(x, p, g, w_gate, w_proj, tm=512):
    t, d = x.shape
    tm = min(tm, t)
    pd = p.shape[1]
    return pl.pallas_call(
        _ple_kernel,
        grid=(t // tm,),
        in_specs=[pl.BlockSpec((tm, d), lambda i: (i, 0)),
                  pl.BlockSpec((tm, pd), lambda i: (i, 0)),
                  pl.BlockSpec((1, d), lambda i: (0, 0)),
                  pl.BlockSpec((d, d), lambda i: (0, 0)),
                  pl.BlockSpec((pd, d), lambda i: (0, 0))],
        out_specs=pl.BlockSpec((tm, d), lambda i: (i, 0)),
        out_shape=jax.ShapeDtypeStruct((t, d), F32),
        compiler_params=_params(("parallel",)),
        name="ple_block",
    )(x, p, g, w_gate, w_proj)


def _split_w_in(w):
    sizes = [D_INNER, CONV_DIM, SSM_HEADS, ATT_WIDTH, ATT_KV_HEADS * ATT_HEAD_DIM,
             ATT_KV_HEADS * ATT_HEAD_DIM, IDX_HEADS * IDX_DIM, IDX_DIM, IDX_HEADS, 2 * w.shape[0]]
    out, acc = [], 0
    for s in sizes:
        out.append(w[:, acc:acc + s])
        acc += s
    return out


def kernel(x, p, positions, norm_mix, w_in, conv_w, conv_b, dt_bias, a_log, d_skip, ssm_norm, q_norm, k_norm,
           w_branch, w_out, norm_ffn, peer_wq, peer_keys, peer_u, peer_v, norm_ple, w_ple_gate, w_ple_proj):
    batch, seq, d = x.shape
    t = batch * seq
    depth = w_in.shape[0]
    tables = rope_tables_t(positions)
    xf = x.reshape(t, d)
    for i in range(depth):
        wz, wxbc, wdt, wq, wk, wv, wqi, wki, wwi, wg = _split_w_in(w_in[i])
        w_dt = jnp.pad(wdt, ((0, 0), (0, LANES - SSM_HEADS))).astype(BF16)
        w_att_t = jnp.transpose(jnp.concatenate([wq, wk, wv, wqi, wki, wwi], axis=1)).astype(BF16)
        g_mix = norm_mix[i].reshape(1, d)

        z = norm_matmul(xf, g_mix, wz.astype(BF16))
        xbc = norm_matmul(xf, g_mix, wxbc.astype(BF16))
        dt_slab = norm_matmul(xf, g_mix, w_dt)
        gates = norm_matmul(xf, g_mix, wg.astype(BF16))
        proj_t = norm_matmul_t(xf, g_mix, w_att_t)

        y_ssm = ssm_branch(z, xbc, dt_slab, conv_w[i], conv_b[i], dt_bias[i], a_log[i], d_skip[i],
                           ssm_norm[i], batch, seq)
        qt, k, v3, qit, ki, wit = dsa_prep(proj_t, tables, q_norm[i], k_norm[i])
        y_att = dsa_attention(ki, k, v3, qit, qt, wit, batch, seq)
        xf = merge_branches(xf, y_ssm, y_att, gates, w_branch[i, :D_INNER].astype(BF16),
                            w_branch[i, D_INNER:].astype(BF16), w_out[i].astype(BF16))

        keys = peer_keys[i].reshape(PEER_HEADS * 2, N_KEYS, PEER_QDIM // 2).astype(BF16)
        h_t, tau, e1, e2 = peer_route(xf, norm_ffn[i].reshape(1, d), jnp.transpose(peer_wq[i]).astype(BF16), keys)
        xf = peer_dense(xf, h_t, tau, e1, e2, peer_u[i].astype(BF16), jnp.transpose(peer_v[i]).astype(BF16))

        xf = ple_block(xf, p[i].reshape(t, -1), norm_ple[i].reshape(1, d),
                       w_ple_gate[i].astype(BF16), w_ple_proj[i].astype(BF16))
    return xf.reshape(batch, seq, d)
```

```python
import functools
import math

import numpy as np
import jax
import jax.numpy as jnp
from jax import lax
from jax.experimental import pallas as pl
from jax.experimental.pallas import tpu as pltpu

F32 = jnp.float32
BF16 = jnp.bfloat16
I32 = jnp.int32

EPS = 1e-6
ROPE_THETA = 10000.0

SSM_HEADS = 32
SSM_HEAD_DIM = 64
D_INNER = SSM_HEADS * SSM_HEAD_DIM
SSM_GROUPS = 4
D_STATE = 128
CONV_WIDTH = 4
CHUNK = 128
CONV_DIM = D_INNER + 2 * SSM_GROUPS * D_STATE
ATT_HEADS = 8
ATT_KV_HEADS = 2
ATT_HEAD_DIM = 128
ATT_WIDTH = ATT_HEADS * ATT_HEAD_DIM
IDX_HEADS = 8
IDX_DIM = 64
TOPK_MAX = 256
Q_BLOCK = 128
KEY_CHUNK = 512
KEY_BITS = 32
PEER_HEADS = 8
N_KEYS = 128
PEER_TOPK = 16
PEER_QDIM = 256

LANES = 128
SUBLANES = 8
VMEM_LIMIT = 56 * 1024 * 1024
INT_MIN = np.int32(-2 ** 31)
NEG_BIG = -1e30

_NT = (((1,), (1,)), ((), ()))


def _params(sem):
    return pltpu.CompilerParams(dimension_semantics=sem, vmem_limit_bytes=VMEM_LIMIT)


def _rms(x, g):
    return x * lax.rsqrt(jnp.mean(x * x, axis=-1, keepdims=True) + EPS) * g


def _sigmoid(x):
    return 1.0 / (1.0 + jnp.exp(-x))


def _split3(a):
    a1 = a.astype(BF16)
    r1 = a - a1.astype(F32)
    a2 = r1.astype(BF16)
    a3 = (r1 - a2.astype(F32)).astype(BF16)
    return a1, a2, a3


def _dot(a, b):
    return jnp.dot(a, b, preferred_element_type=F32)


def _norm_mm_kernel(x_ref, g_ref, w_ref, o_ref):
    h = _rms(x_ref[...], g_ref[...]).astype(BF16)
    o_ref[...] = _dot(h, w_ref[...]).astype(o_ref.dtype)


def norm_matmul(x, g, w, out_dtype=F32, tm=512):
    m, k = x.shape
    n = w.shape[1]
    return pl.pallas_call(
        _norm_mm_kernel,
        grid=(m // tm,),
        in_specs=[pl.BlockSpec((tm, k), lambda i: (i, 0)),
                  pl.BlockSpec((1, k), lambda i: (0, 0)),
                  pl.BlockSpec((k, n), lambda i: (0, 0))],
        out_specs=pl.BlockSpec((tm, n), lambda i: (i, 0)),
        out_shape=jax.ShapeDtypeStruct((m, n), out_dtype),
        compiler_params=_params(("parallel",)),
        name="norm_matmul",
    )(x, g, w)


def _norm_mm_t_kernel(x_ref, g_ref, wt_ref, o_ref):
    h = _rms(x_ref[...], g_ref[...]).astype(BF16)
    o_ref[...] = lax.dot_general(wt_ref[...], h, _NT, preferred_element_type=F32)


def norm_matmul_t(x, g, wt, tm=512):
    m, k = x.shape
    n = wt.shape[0]
    return pl.pallas_call(
        _norm_mm_t_kernel,
        grid=(m // tm,),
        in_specs=[pl.BlockSpec((tm, k), lambda i: (i, 0)),
                  pl.BlockSpec((1, k), lambda i: (0, 0)),
                  pl.BlockSpec((n, k), lambda i: (0, 0))],
        out_specs=pl.BlockSpec((n, tm), lambda i: (0, i)),
        out_shape=jax.ShapeDtypeStruct((n, m), F32),
        compiler_params=_params(("parallel",)),
        name="norm_matmul_t",
    )(x, g, wt)


def _ssm_kernel(zx_ref, dt_ref, convw_ref, convb_ref, dtb_ref, alog_ref,
                dskip_ref, normg_ref, expand_ref, y_ref, carry_ref, state_ref):
    @pl.when(pl.program_id(1) == 0)
    def _():
        carry_ref[...] = jnp.zeros_like(carry_ref)
        state_ref[...] = jnp.zeros_like(state_ref)

    xbc = zx_ref[:, D_INNER:]
    ext = jnp.concatenate([carry_ref[...], xbc], axis=0)
    conv = convb_ref[...]
    for j in range(CONV_WIDTH):
        lo = SUBLANES - (CONV_WIDTH - 1) + j
        conv = conv + convw_ref[j:j + 1, :] * ext[lo:lo + CHUNK]
    carry_ref[...] = xbc[CHUNK - SUBLANES:]
    xc = conv * _sigmoid(conv)
    xs = xc[:, :D_INNER]
    gw = SSM_GROUPS * D_STATE
    bm = xc[:, D_INNER:D_INNER + gw]
    cm = xc[:, D_INNER + gw:]

    lane = lax.broadcasted_iota(I32, (CHUNK, LANES), 1)
    row = lax.broadcasted_iota(I32, (CHUNK, LANES), 0)
    head_lane = lane < SSM_HEADS
    dt = jax.nn.softplus(dt_ref[...] + dtb_ref[...])
    adt = jnp.where(head_lane, dt * (-jnp.exp(alog_ref[...])), 0.0)
    tril = jnp.where(row >= lane, 1.0, 0.0).astype(BF16)
    cs = sum(_dot(tril, part) for part in _split3(adt))
    cs_row = cs.T
    cs_last = cs[CHUNK - 1:CHUNK, :]
    ecs = jnp.exp(cs)
    decay = jnp.exp(cs_last - cs)

    expand = expand_ref[...]

    def widen(a):
        return sum(_dot(part, expand) for part in _split3(a))

    dt_w = widen(dt)
    ecs_w = widen(ecs)
    decay_w = widen(decay)
    xdt = xs * dt_w
    xdt_b = xdt.astype(BF16)
    xdec_b = (xdt * decay_w).astype(BF16)
    causal = row >= lane
    pair_lo = lane < SSM_HEAD_DIM

    gh = SSM_HEADS // SSM_GROUPS
    gcols = gh * SSM_HEAD_DIM
    ys = []
    for g in range(SSM_GROUPS):
        bg = bm[:, g * D_STATE:(g + 1) * D_STATE]
        cg = cm[:, g * D_STATE:(g + 1) * D_STATE].astype(BF16)
        cb = lax.dot_general(cg, bg.astype(BF16), _NT, preferred_element_type=F32)
        st = state_ref[g]
        y_off = _dot(cg, st.astype(BF16))
        new_st = _dot(bg.T.astype(BF16), xdec_b[:, g * gcols:(g + 1) * gcols])
        state_ref[g] = st * ecs_w[CHUNK - 1:CHUNK, g * gcols:(g + 1) * gcols] + new_st
        pieces = []
        for pr in range(gh // 2):
            outs = []
            for sub in range(2):
                h = g * gh + pr * 2 + sub
                lm = jnp.where(causal, jnp.exp(cs[:, h:h + 1] - cs_row[h:h + 1, :]), 0.0)
                mm = (cb * lm).astype(BF16)
                c0 = (h // 2) * LANES
                outs.append(_dot(mm, xdt_b[:, c0:c0 + LANES]))
            pieces.append(jnp.where(pair_lo, outs[0], outs[1]))
        y_diag = jnp.concatenate(pieces, axis=1)
        ys.append(y_diag + y_off * ecs_w[:, g * gcols:(g + 1) * gcols])
    y = jnp.concatenate(ys, axis=1) + dskip_ref[...] * xs
    z = zx_ref[:, :D_INNER]
    y = y * (z * _sigmoid(z))
    outs = []
    for g in range(SSM_GROUPS):
        yg = y[:, g * gcols:(g + 1) * gcols]
        outs.append(yg * lax.rsqrt(jnp.mean(yg * yg, axis=-1, keepdims=True) + EPS))
    y_ref[...] = (jnp.concatenate(outs, axis=1) * normg_ref[...]).astype(y_ref.dtype)


def ssm_branch(zx, gates_dt, conv_w, conv_b, dt_bias, a_log, d_skip, norm_g, batch, seq):
    t = batch * seq
    nchunk = seq // CHUNK
    pad = LANES - SSM_HEADS
    convw_t = jnp.transpose(conv_w)
    dtb = jnp.pad(dt_bias, (0, pad)).reshape(1, LANES)
    alog = jnp.pad(a_log, (0, pad)).reshape(1, LANES)
    dskip = jnp.repeat(d_skip, SSM_HEAD_DIM).reshape(1, D_INNER)
    expand = (np.arange(LANES)[:, None] == (np.arange(D_INNER)[None, :] // SSM_HEAD_DIM))
    expand = jnp.asarray(expand, dtype=BF16)
    tok = lambda w: pl.BlockSpec((CHUNK, w), lambda b, c: (b * nchunk + c, 0))
    const = lambda r, w: pl.BlockSpec((r, w), lambda b, c: (0, 0))
    dt_block = (gates_dt.shape[1] - LANES) // LANES
    return pl.pallas_call(
        _ssm_kernel,
        grid=(batch, nchunk),
        in_specs=[tok(D_INNER + CONV_DIM), pl.BlockSpec((CHUNK, LANES), lambda b, c: (b * nchunk + c, dt_block)),
                  const(CONV_WIDTH, CONV_DIM), const(1, CONV_DIM), const(1, LANES),
                  const(1, LANES), const(1, D_INNER), const(1, D_INNER),
                  const(LANES, D_INNER)],
        out_specs=tok(D_INNER),
        out_shape=jax.ShapeDtypeStruct((t, D_INNER), BF16),
        scratch_shapes=[pltpu.VMEM((SUBLANES, CONV_DIM), F32),
                        pltpu.VMEM((SSM_GROUPS, D_STATE, D_INNER // SSM_GROUPS), F32)],
        compiler_params=_params(("parallel", "arbitrary")),
        name="ssm_branch",
    )(zx, gates_dt, convw_t, conv_b.reshape(1, CONV_DIM), dtb, alog, dskip,
      norm_g.reshape(1, D_INNER), expand)


def _rope_kernel(pos_ref, inva_ref, invi_ref, ca_ref, sa_ref, ci_ref, si_ref):
    pos = pos_ref[...]
    ang_a = inva_ref[...] * pos
    ang_i = invi_ref[...] * pos
    ca_ref[...] = jnp.cos(ang_a)
    sa_ref[...] = jnp.sin(ang_a)
    ci_ref[...] = jnp.cos(ang_i)
    si_ref[...] = jnp.sin(ang_i)


def rope_tables_t(positions, tile=2048):
    t = positions.size
    tile = min(tile, t)
    pos = positions.reshape(1, t).astype(F32)
    ha, hi = ATT_HEAD_DIM // 2, IDX_DIM // 2
    inv_a = (1.0 / (ROPE_THETA ** (jnp.arange(0, ATT_HEAD_DIM, 2, dtype=F32) / ATT_HEAD_DIM))).reshape(ha, 1)
    inv_i = (1.0 / (ROPE_THETA ** (jnp.arange(0, IDX_DIM, 2, dtype=F32) / IDX_DIM))).reshape(hi, 1)
    col = lambda r: pl.BlockSpec((r, tile), lambda i: (0, i))
    return pl.pallas_call(
        _rope_kernel,
        grid=(t // tile,),
        in_specs=[col(1), pl.BlockSpec((ha, 1), lambda i: (0, 0)), pl.BlockSpec((hi, 1), lambda i: (0, 0))],
        out_specs=[col(ha), col(ha), col(hi), col(hi)],
        out_shape=[jax.ShapeDtypeStruct((ha, t), F32)] * 2 + [jax.ShapeDtypeStruct((hi, t), F32)] * 2,
        compiler_params=_params(("parallel",)),
        name="rope_tables",
    )(pos, inv_a, inv_i)


_Q0 = 0
_K0 = _Q0 + ATT_WIDTH
_V0 = _K0 + ATT_KV_HEADS * ATT_HEAD_DIM
_QI0 = _V0 + ATT_KV_HEADS * ATT_HEAD_DIM
_KI0 = _QI0 + IDX_HEADS * IDX_DIM
_WI0 = _KI0 + IDX_DIM
_ATT_ROWS = _WI0 + IDX_HEADS


def _rope_rows(x, cos, sin):
    half = x.shape[0] // 2
    x1, x2 = x[:half], x[half:]
    return jnp.concatenate([x1 * cos - x2 * sin, x1 * sin + x2 * cos], axis=0)


def _head_norm_rows(x, g):
    return x * lax.rsqrt(jnp.mean(x * x, axis=0, keepdims=True) + EPS) * g


def _dsa_prep_kernel(p_ref, ca_ref, sa_ref, ci_ref, si_ref, qg_ref, kg_ref,
                     qt_ref, k_ref, v_ref, qit_ref, ki_ref, wit_ref):
    tile = p_ref.shape[1]
    ca, sa, ci, si = ca_ref[...], sa_ref[...], ci_ref[...], si_ref[...]
    hd = ATT_HEAD_DIM
    for h in range(ATT_HEADS):
        x = _head_norm_rows(p_ref[_Q0 + h * hd:_Q0 + (h + 1) * hd, :], qg_ref[...])
        qt_ref[h * hd:(h + 1) * hd, :] = (_rope_rows(x, ca, sa) * (hd ** -0.5)).astype(BF16)
    for h in range(ATT_KV_HEADS):
        x = _head_norm_rows(p_ref[_K0 + h * hd:_K0 + (h + 1) * hd, :], kg_ref[...])
        k_ref[:, h * hd:(h + 1) * hd] = _rope_rows(x, ca, sa).T.astype(BF16)
    for s in range(tile // KEY_CHUNK):
        v_ref[s] = p_ref[_V0:_V0 + ATT_KV_HEADS * hd, s * KEY_CHUNK:(s + 1) * KEY_CHUNK].astype(BF16)
    for h in range(IDX_HEADS):
        x = _rope_rows(p_ref[_QI0 + h * IDX_DIM:_QI0 + (h + 1) * IDX_DIM, :], ci, si) * (IDX_DIM ** -0.5)
        x = x.astype(BF16)
        for s in range(tile // Q_BLOCK):
            qit_ref[s, :, h * Q_BLOCK:(h + 1) * Q_BLOCK] = x[:, s * Q_BLOCK:(s + 1) * Q_BLOCK]
    x = _rope_rows(p_ref[_KI0:_KI0 + IDX_DIM, :], ci, si)
    zpad = jnp.zeros((LANES - IDX_DIM, tile), F32)
    ki_ref[...] = jnp.concatenate([x, zpad], axis=0).T[:, :IDX_DIM].astype(BF16)
    wit_ref[...] = p_ref[_WI0:_WI0 + IDX_HEADS, :] * (IDX_HEADS ** -0.5)


def dsa_prep(proj_t, tables, q_norm, k_norm, tile=512):
    t = proj_t.shape[1]
    tile = min(tile, t)
    ca, sa, ci, si = tables
    ha, hi = ATT_HEAD_DIM // 2, IDX_DIM // 2
    col = lambda r: pl.BlockSpec((r, tile), lambda i: (0, i))
    rowb = lambda w: pl.BlockSpec((tile, w), lambda i: (i, 0))
    const = lambda r, w: pl.BlockSpec((r, w), lambda i: (0, 0))
    kvw = ATT_KV_HEADS * ATT_HEAD_DIM
    return pl.pallas_call(
        _dsa_prep_kernel,
        grid=(t // tile,),
        in_specs=[col(_ATT_ROWS), col(ha), col(ha), col(hi), col(hi),
                  const(ATT_HEAD_DIM, 1), const(ATT_HEAD_DIM, 1)],
        out_specs=[col(ATT_WIDTH), rowb(kvw),
                   pl.BlockSpec((tile // KEY_CHUNK, kvw, KEY_CHUNK), lambda i: (i, 0, 0)),
                   pl.BlockSpec((tile // Q_BLOCK, IDX_DIM, IDX_HEADS * Q_BLOCK), lambda i: (i, 0, 0)),
                   rowb(IDX_DIM), col(IDX_HEADS)],
        out_shape=[jax.ShapeDtypeStruct((ATT_WIDTH, t), BF16),
                   jax.ShapeDtypeStruct((t, kvw), BF16),
                   jax.ShapeDtypeStruct((t // KEY_CHUNK, kvw, KEY_CHUNK), BF16),
                   jax.ShapeDtypeStruct((t // Q_BLOCK, IDX_DIM, IDX_HEADS * Q_BLOCK), BF16),
                   jax.ShapeDtypeStruct((t, IDX_DIM), BF16),
                   jax.ShapeDtypeStruct((IDX_HEADS, t), F32)],
        compiler_params=_params(("parallel",)),
        name="dsa_prep",
    )(proj_t, ca, sa, ci, si, q_norm.reshape(ATT_HEAD_DIM, 1), k_norm.reshape(ATT_HEAD_DIM, 1))


def _bit_transpose32(words):
    a = list(words)
    for j, mask in ((16, 0x0000FFFF), (8, 0x00FF00FF), (4, 0x0F0F0F0F), (2, 0x33333333), (1, 0x55555555)):
        for k in range(32):
            if k & j == 0:
                t = (lax.shift_right_logical(a[k], np.int32(j)) ^ a[k + j]) & np.int32(mask)
                a[k + j] = a[k + j] ^ t
                a[k] = a[k] ^ lax.shift_left(t, np.int32(j))
    return a


def _dsa_kernel(ki_ref, k_ref, v_ref, qit_ref, qt_ref, wit_ref, o_ref, keys_ref, bias_ref, acc_ref, s_ref,
                planes_ref, *, topk, idx_bits):
    qb = Q_BLOCK
    kc = KEY_CHUNK
    j = pl.program_id(1)
    nk = (j * qb + qb + kc - 1) // kc
    row = lax.broadcasted_iota(I32, (kc, qb), 0)
    col = lax.broadcasted_iota(I32, (kc, qb), 1)
    q_pos = j * qb + col
    prows = kc // KEY_BITS

    def score_chunk(c, carry):
        ks = pl.multiple_of(c * kc, kc)
        s = _dot(ki_ref[pl.ds(ks, kc), :], qit_ref[0])
        acc = jnp.zeros((kc, qb), F32)
        for h in range(IDX_HEADS):
            acc = acc + wit_ref[h:h + 1, :] * jnp.maximum(s[:, h * qb:(h + 1) * qb], 0.0)
        acc = jnp.where(acc == 0.0, 0.0, acc)
        bits = pltpu.bitcast(acc, I32)
        key = jnp.where(bits < 0, bits ^ np.int32(0x7FFFFFFF), bits)
        key = jnp.where(ks + row <= q_pos, key, INT_MIN)
        keys_ref[pl.ds(ks, kc), :] = key
        u = key ^ INT_MIN
        for sub in range(prows // SUBLANES):
            r0 = sub * KEY_BITS * SUBLANES
            planes = _bit_transpose32([u[r0 + i * SUBLANES:r0 + (i + 1) * SUBLANES] for i in range(KEY_BITS)])
            ps = pl.multiple_of(c * prows + sub * SUBLANES, SUBLANES)
            for b in range(KEY_BITS):
                planes_ref[b, pl.ds(ps, SUBLANES), :] = planes[b]
        return carry

    lax.fori_loop(0, nk, score_chunk, 0)

    def clear_chunk(c, carry):
        ps = pl.multiple_of(c * prows, prows)
        for b in range(KEY_BITS):
            planes_ref[b, pl.ds(ps, prows), :] = jnp.zeros((prows, qb), I32)
        return carry

    lax.fori_loop(nk, planes_ref.shape[1] // prows, clear_chunk, 0)

    def count(hit):
        def body(c, acc):
            ks = pl.multiple_of(c * kc, kc)
            one = hit(keys_ref[pl.ds(ks, kc), :], ks + row)
            return acc + one.reshape(kc // SUBLANES, SUBLANES, qb).sum(axis=0)
        acc = lax.fori_loop(0, nk, body, jnp.zeros((SUBLANES, qb), I32))
        return acc.sum(axis=0, keepdims=True)

    nrows = planes_ref.shape[1]
    prow = lax.broadcasted_iota(I32, (nrows, qb), 0)
    active0 = jnp.where(prow < nk * prows, np.int32(-1), np.int32(0))

    def value_bit(i, carry):
        active, need, prefix = carry
        hit = active & planes_ref[KEY_BITS - 1 - i]
        n = jnp.sum(lax.population_count(hit), axis=0, keepdims=True)
        take = n >= need
        active = jnp.where(take, hit, active ^ hit)
        need = jnp.where(take, need, need - n)
        prefix = jnp.where(take, prefix | jnp.left_shift(np.int32(1), KEY_BITS - 1 - i), prefix)
        return active, need, prefix

    active, need, prefix = lax.fori_loop(
        0, KEY_BITS, value_bit, (active0, jnp.full((1, qb), topk, I32), jnp.zeros((1, qb), I32)))
    thr = prefix ^ INT_MIN

    n_tied = jnp.sum(lax.population_count(active), axis=0, keepdims=True)
    excess = jnp.max(jnp.where(thr > INT_MIN, n_tied - need, 0))

    def tie_search():
        def index_bit(i, prefix):
            cand = prefix | jnp.left_shift(np.int32(1), idx_bits - 1 - i)
            n = count(lambda key, idx: jnp.where(key == thr, jnp.where(idx < cand, 1, 0), 0))
            return jnp.where(n < need, cand, prefix)
        return lax.fori_loop(0, idx_bits, index_bit, jnp.zeros((1, qb), I32))

    last = lax.cond(excess > 0, tie_search, lambda: jnp.full((1, qb), 2 ** idx_bits, I32))

    def bias_chunk(c, carry):
        ks = pl.multiple_of(c * kc, kc)
        key = keys_ref[pl.ds(ks, kc), :]
        idx = ks + row
        tied = jnp.where(key == thr, jnp.where(idx <= last, 0.0, NEG_BIG), NEG_BIG)
        bias = jnp.where(key > thr, 0.0, tied)
        bias_ref[pl.ds(ks, kc), :] = jnp.where(idx <= q_pos, bias, NEG_BIG)
        return carry

    lax.fori_loop(0, nk, bias_chunk, 0)

    rep = ATT_HEADS // ATT_KV_HEADS
    hd = ATT_HEAD_DIM
    wide = rep * qb
    groups = range(ATT_KV_HEADS)
    qgs = [jnp.concatenate([qt_ref[(g * rep + r) * hd:(g * rep + r + 1) * hd, :] for r in range(rep)], axis=1)
           for g in groups]

    def fold(x, op):
        return op(x.reshape(kc // SUBLANES, SUBLANES, wide), axis=0)

    def score_sweep(c, carry):
        ks = pl.multiple_of(c * kc, kc)
        b = bias_ref[pl.ds(ks, kc), :]
        b = jnp.concatenate([b] * rep, axis=1)
        out = []
        for g in groups:
            s = _dot(k_ref[pl.ds(ks, kc), g * hd:(g + 1) * hd], qgs[g]) + b
            s_ref[g, pl.ds(ks, kc), :] = s
            out.append(jnp.maximum(carry[g], fold(s, jnp.max)))
        return tuple(out)

    tops = lax.fori_loop(0, nk, score_sweep, tuple(jnp.full((SUBLANES, wide), NEG_BIG, F32) for _ in groups))
    tops = [jnp.max(m, axis=0, keepdims=True) for m in tops]
    acc_ref[...] = jnp.zeros_like(acc_ref)

    def value_sweep(c, carry):
        ks = pl.multiple_of(c * kc, kc)
        out = []
        for g in groups:
            p = jnp.exp(s_ref[g, pl.ds(ks, kc), :] - tops[g])
            acc_ref[g] += _dot(v_ref[c, g * hd:(g + 1) * hd, :], p.astype(BF16))
            out.append(carry[g] + fold(p, jnp.sum))
        return tuple(out)

    sums = lax.fori_loop(0, nk, value_sweep, tuple(jnp.zeros((SUBLANES, wide), F32) for _ in groups))
    for g in groups:
        o = acc_ref[g] / jnp.sum(sums[g], axis=0, keepdims=True)
        for r in range(rep):
            h = g * rep + r
            o_ref[:, h * hd:(h + 1) * hd] = o[:, r * qb:(r + 1) * qb].T.astype(o_ref.dtype)


def dsa_attention(ki, k, v3, qit, qt, wit, batch, seq):
    t = batch * seq
    nqb = seq // Q_BLOCK
    kvw = ATT_KV_HEADS * ATT_HEAD_DIM
    topk = min(TOPK_MAX, seq // 4)
    idx_bits = max(1, int(math.ceil(math.log2(seq))))
    assert seq % KEY_CHUNK == 0 and KEY_CHUNK % (KEY_BITS * SUBLANES) == 0 and KEY_CHUNK % Q_BLOCK == 0
    qcol = lambda r: pl.BlockSpec((r, Q_BLOCK), lambda b, j: (0, b * nqb + j))
    return pl.pallas_call(
        functools.partial(_dsa_kernel, topk=topk, idx_bits=idx_bits),
        grid=(batch, nqb),
        in_specs=[pl.BlockSpec((seq, IDX_DIM), lambda b, j: (b, 0)),
                  pl.BlockSpec((seq, kvw), lambda b, j: (b, 0)),
                  pl.BlockSpec((seq // KEY_CHUNK, kvw, KEY_CHUNK), lambda b, j: (b, 0, 0)),
                  pl.BlockSpec((1, IDX_DIM, IDX_HEADS * Q_BLOCK), lambda b, j: (b * nqb + j, 0, 0)),
                  qcol(ATT_WIDTH), qcol(IDX_HEADS)],
        out_specs=pl.BlockSpec((Q_BLOCK, ATT_WIDTH), lambda b, j: (b * nqb + j, 0)),
        out_shape=jax.ShapeDtypeStruct((t, ATT_WIDTH), BF16),
        scratch_shapes=[pltpu.VMEM((seq, Q_BLOCK), I32), pltpu.VMEM((seq, Q_BLOCK), F32),
                        pltpu.VMEM((ATT_KV_HEADS, ATT_HEAD_DIM, (ATT_HEADS // ATT_KV_HEADS) * Q_BLOCK), F32),
                        pltpu.VMEM((ATT_KV_HEADS, seq, (ATT_HEADS // ATT_KV_HEADS) * Q_BLOCK), F32),
                        pltpu.VMEM((KEY_BITS, seq // KEY_BITS, Q_BLOCK), I32)],
        compiler_params=_params(("parallel", "arbitrary")),
        name="dsa_attention",
    )(ki, k, v3, qit, qt, wit)


def _merge_kernel(x_ref, ys_ref, ya_ref, gate_ref, wbs_ref, wba_ref, wo_ref, o_ref):
    d = x_ref.shape[1]
    gates = gate_ref[...]
    merged = (_sigmoid(gates[:, :d]) * _dot(ys_ref[...], wbs_ref[...])
              + _sigmoid(gates[:, d:]) * _dot(ya_ref[...], wba_ref[...]))
    o_ref[...] = x_ref[...] + _dot(merged.astype(BF16), wo_ref[...])


def merge_branches(x, y_ssm, y_att, gates, w_bs, w_ba, w_o, tm=512):
    t, d = x.shape
    tm = min(tm, t)
    tok = lambda w: pl.BlockSpec((tm, w), lambda i: (i, 0))
    const = lambda r, w: pl.BlockSpec((r, w), lambda i: (0, 0))
    return pl.pallas_call(
        _merge_kernel,
        grid=(t // tm,),
        in_specs=[tok(d), tok(D_INNER), tok(ATT_WIDTH), tok(2 * d),
                  const(D_INNER, d), const(ATT_WIDTH, d), const(d, d)],
        out_specs=tok(d),
        out_shape=jax.ShapeDtypeStruct((t, d), F32),
        compiler_params=_params(("parallel",)),
        name="merge_branches",
    )(x, y_ssm, y_att, gates, w_bs, w_ba, w_o)


def _top_rows(s, k, rows_out):
    rows = lax.broadcasted_iota(I32, (rows_out, s.shape[1]), 0)

    def body(r, carry):
        s, top = carry
        m = jnp.max(s, axis=0, keepdims=True)
        return jnp.where(s == m, -jnp.inf, s), jnp.where(rows == r, m, top)

    _, top = lax.fori_loop(0, k, body, (s, jnp.full((rows_out, s.shape[1]), -jnp.inf, F32)))
    return top


def _peer_route_kernel(x_ref, g_ref, wqt_ref, keys_ref, ht_ref, tau_ref, e1_ref, e2_ref):
    tm = x_ref.shape[0]
    h = _rms(x_ref[...], g_ref[...])
    ht_ref[...] = h.T.astype(BF16)
    qt = lax.dot_general(wqt_ref[...], h.astype(BF16), _NT, preferred_element_type=F32).astype(BF16)
    half = PEER_QDIM // 2
    k = PEER_TOPK
    k1 = k + 1
    rows_out = -(-k1 // SUBLANES) * SUBLANES
    sub8 = lax.broadcasted_iota(I32, (SUBLANES, tm), 0)
    for hh in range(PEER_HEADS):
        s1 = _dot(keys_ref[2 * hh], qt[(2 * hh) * half:(2 * hh + 1) * half, :])
        s2 = _dot(keys_ref[2 * hh + 1], qt[(2 * hh + 1) * half:(2 * hh + 2) * half, :])
        a = _top_rows(s1, k1, rows_out)
        b = _top_rows(s2, k1, rows_out)
        assert k1 // (SUBLANES + 1) <= 1
        tiles = [a[r:r + SUBLANES] + b[0:1] for r in range(SUBLANES, rows_out, SUBLANES)]
        for i in range(SUBLANES):
            lim = k1 // (i + 1)
            for r in range(0, lim, SUBLANES):
                tile = b[r:r + SUBLANES] + a[i:i + 1]
                tiles.append(tile if lim - r >= SUBLANES else jnp.where(sub8 < lim - r, tile, -jnp.inf))
        cand = jnp.concatenate(tiles, axis=0)
        best = a[0:1] + b[0:1]

        def body(r, carry):
            cand, zsum, kth, nxt = carry
            m = jnp.max(cand, axis=0, keepdims=True)
            zsum = zsum + jnp.where(r < k, jnp.exp(m - best), 0.0)
            return jnp.where(cand == m, -jnp.inf, cand), zsum, jnp.where(r == k - 1, m, kth), m

        _, zsum, kth, nxt = lax.fori_loop(0, k1, body, (cand, jnp.zeros((1, tm), F32), best, best))
        cut = 0.5 * (kth + nxt)
        r0 = hh * N_KEYS
        tau_ref[r0:r0 + N_KEYS, :] = jnp.exp((cut - b[0:1]) - s1)
        e1_ref[r0:r0 + N_KEYS, :] = jnp.exp(s1 - a[0:1]) * (0.5 / zsum)
        e2_ref[r0:r0 + N_KEYS, :] = jnp.exp(s2 - b[0:1])


def peer_route(x, g, wq_t, keys, tm=256):
    t, d = x.shape
    tm = min(tm, t)
    rows = PEER_HEADS * N_KEYS
    col = lambda r: pl.BlockSpec((r, tm), lambda i: (0, i))
    return pl.pallas_call(
        _peer_route_kernel,
        grid=(t // tm,),
        in_specs=[pl.BlockSpec((tm, d), lambda i: (i, 0)),
                  pl.BlockSpec((1, d), lambda i: (0, 0)),
                  pl.BlockSpec(wq_t.shape, lambda i: (0, 0)),
                  pl.BlockSpec(keys.shape, lambda i: (0, 0, 0))],
        out_specs=[col(d), col(rows), col(rows), col(rows)],
        out_shape=[jax.ShapeDtypeStruct((d, t), BF16)] + [jax.ShapeDtypeStruct((rows, t), F32)] * 3,
        compiler_params=_params(("parallel",)),
        name="peer_route",
    )(x, g, wq_t, keys)


PEER_ROWS = 32


def _peer_dense_kernel(x_ref, ht_ref, tau_ref, e1_ref, e2_ref, u_ref, vt_ref, o_ref, acc_ref, hid_ref, act_ref,
                       *, ib, nblk):
    j = pl.program_id(1)
    tt = ht_ref.shape[1]
    cur = j % 2
    prev = 1 - cur

    @pl.when(j == 0)
    def _():
        acc_ref[...] = jnp.zeros_like(acc_ref)
        hid_ref[...] = jnp.zeros_like(hid_ref)
        act_ref[...] = jnp.zeros_like(act_ref)

    acc_ref[...] += _dot(vt_ref[...], act_ref[cur])
    blk = jnp.clip(j - 1, 0, nblk - 1)
    for ii in range(ib):
        i1 = blk * ib + ii
        taus = [tau_ref[pl.ds(hh * N_KEYS + i1, 1), :] for hh in range(PEER_HEADS)]
        e1s = [e1_ref[pl.ds(hh * N_KEYS + i1, 1), :] for hh in range(PEER_HEADS)]
        for lc in range(tt // LANES):
            cols = slice(lc * LANES, (lc + 1) * LANES)
            tau_b = [jnp.broadcast_to(taus[hh][:, cols], (PEER_ROWS, LANES)) for hh in range(PEER_HEADS)]
            e1_b = [jnp.broadcast_to(e1s[hh][:, cols], (PEER_ROWS, LANES)) for hh in range(PEER_HEADS)]
            for rb in range(N_KEYS // PEER_ROWS):
                gate = jnp.zeros((PEER_ROWS, LANES), F32)
                for hh in range(PEER_HEADS):
                    r0 = hh * N_KEYS + rb * PEER_ROWS
                    e2 = e2_ref[r0:r0 + PEER_ROWS, cols]
                    gate = gate + jnp.where(e2 >= tau_b[hh], e2, 0.0) * e1_b[hh]
                rows = slice(ii * N_KEYS + rb * PEER_ROWS, ii * N_KEYS + (rb + 1) * PEER_ROWS)
                hb = hid_ref[prev, rows, cols]
                act = hb * (1.0 + lax.erf(hb * np.float32(math.sqrt(0.5))))
                act_ref[prev, rows, cols] = (act * gate).astype(BF16)
    hid_ref[cur] = _dot(u_ref[...], ht_ref[...])

    @pl.when(j == pl.num_programs(1) - 1)
    def _():
        o_ref[...] = x_ref[...] + acc_ref[...].T


def peer_dense(x, h_t, tau, e1, e2, u, v_t, tt=512, ib=4):
    t, d = x.shape
    tt = min(tt, t)
    rows = PEER_HEADS * N_KEYS
    nblk = N_KEYS // ib
    col = lambda r: pl.BlockSpec((r, tt), lambda i, j: (0, i))
    tok = pl.BlockSpec((tt, d), lambda i, j: (i, 0))
    return pl.pallas_call(
        functools.partial(_peer_dense_kernel, ib=ib, nblk=nblk),
        grid=(t // tt, nblk + 2),
        in_specs=[tok, col(d), col(rows), col(rows), col(rows),
                  pl.BlockSpec((ib * N_KEYS, d), lambda i, j: (jnp.minimum(j, nblk - 1), 0)),
                  pl.BlockSpec((d, ib * N_KEYS), lambda i, j: (0, jnp.clip(j - 2, 0, nblk - 1)))],
        out_specs=tok,
        out_shape=jax.ShapeDtypeStruct((t, d), F32),
        scratch_shapes=[pltpu.VMEM((d, tt), F32), pltpu.VMEM((2, ib * N_KEYS, tt), F32),
                        pltpu.VMEM((2, ib * N_KEYS, tt), BF16)],
        compiler_params=_params(("parallel", "arbitrary")),
        name="peer_dense",
    )(x, h_t, tau, e1, e2, u, v_t)


def _ple_kernel(x_ref, p_ref, g_ref, wg_ref, wp_ref, o_ref):
    x = x_ref[...]
    h = _rms(x, g_ref[...]).astype(BF16)
    gate = _sigmoid(_dot(h, wg_ref[...]))
    o_ref[...] = x + gate * _dot(p_ref[...].astype(BF16), wp_ref[...])


def ple_block(x, p, g, w_gate, w_proj, tm=512):
    t, d = x.shape
    tm = min(tm, t)
    pd = p.shape[1]
    return pl.pallas_call(
        _ple_kernel,
        grid=(t // tm,),
        in_specs=[pl.BlockSpec((tm, d), lambda i: (i, 0)),
                  pl.BlockSpec((tm, pd), lambda i: (i, 0)),
                  pl.BlockSpec((1, d), lambda i: (0, 0)),
                  pl.BlockSpec((d, d), lambda i: (0, 0)),
                  pl.BlockSpec((pd, d), lambda i: (0, 0))],
        out_specs=pl.BlockSpec((tm, d), lambda i: (i, 0)),
        out_shape=jax.ShapeDtypeStruct((t, d), F32),
        compiler_params=_params(("parallel",)),
        name="ple_block",
    )(x, p, g, w_gate, w_proj)


def _split_w_in(w):
    sizes = [D_INNER, CONV_DIM, SSM_HEADS, ATT_WIDTH, ATT_KV_HEADS * ATT_HEAD_DIM,
             ATT_KV_HEADS * ATT_HEAD_DIM, IDX_HEADS * IDX_DIM, IDX_DIM, IDX_HEADS, 2 * w.shape[0]]
    out, acc = [], 0
    for s in sizes:
        out.append(w[:, acc:acc + s])
        acc += s
    return out


def kernel(x, p, positions, norm_mix, w_in, conv_w, conv_b, dt_bias, a_log, d_skip, ssm_norm, q_norm, k_norm,
           w_branch, w_out, norm_ffn, peer_wq, peer_keys, peer_u, peer_v, norm_ple, w_ple_gate, w_ple_proj):
    batch, seq, d = x.shape
    t = batch * seq
    depth = w_in.shape[0]
    tables = rope_tables_t(positions)
    xf = x.reshape(t, d)
    for i in range(depth):
        wz, wxbc, wdt, wq, wk, wv, wqi, wki, wwi, wg = _split_w_in(w_in[i])
        w_dt = jnp.pad(wdt, ((0, 0), (0, LANES - SSM_HEADS))).astype(BF16)
        w_att_t = jnp.transpose(jnp.concatenate([wq, wk, wv, wqi, wki, wwi], axis=1)).astype(BF16)
        g_mix = norm_mix[i].reshape(1, d)

        zx = norm_matmul(xf, g_mix, jnp.concatenate([wz, wxbc], axis=1).astype(BF16))
        gates_dt = norm_matmul(xf, g_mix, jnp.concatenate([wg.astype(BF16), w_dt], axis=1))
        proj_t = norm_matmul_t(xf, g_mix, w_att_t)

        y_ssm = ssm_branch(zx, gates_dt, conv_w[i], conv_b[i], dt_bias[i], a_log[i], d_skip[i],
                           ssm_norm[i], batch, seq)
        qt, k, v3, qit, ki, wit = dsa_prep(proj_t, tables, q_norm[i], k_norm[i])
        y_att = dsa_attention(ki, k, v3, qit, qt, wit, batch, seq)
        xf = merge_branches(xf, y_ssm, y_att, gates_dt, w_branch[i, :D_INNER].astype(BF16),
                            w_branch[i, D_INNER:].astype(BF16), w_out[i].astype(BF16))

        keys = peer_keys[i].reshape(PEER_HEADS * 2, N_KEYS, PEER_QDIM // 2).astype(BF16)
        h_t, tau, e1, e2 = peer_route(xf, norm_ffn[i].reshape(1, d), jnp.transpose(peer_wq[i]).astype(BF16), keys)
        xf = peer_dense(xf, h_t, tau, e1, e2, peer_u[i].astype(BF16), jnp.transpose(peer_v[i]).astype(BF16))

        xf = ple_block(xf, p[i].reshape(t, -1), norm_ple[i].reshape(1, d),
                       w_ple_gate[i].astype(BF16), w_ple_proj[i].astype(BF16))
    return xf.reshape(batch, seq, d)
```

```python
import functools
import math

import numpy as np
import jax
import jax.numpy as jnp
from jax import lax
from jax.experimental import pallas as pl
from jax.experimental.pallas import tpu as pltpu

F32 = jnp.float32
BF16 = jnp.bfloat16
I32 = jnp.int32

EPS = 1e-6
ROPE_THETA = 10000.0

SSM_HEADS = 32
SSM_HEAD_DIM = 64
D_INNER = SSM_HEADS * SSM_HEAD_DIM
SSM_GROUPS = 4
D_STATE = 128
CONV_WIDTH = 4
CHUNK = 128
CONV_DIM = D_INNER + 2 * SSM_GROUPS * D_STATE
ATT_HEADS = 8
ATT_KV_HEADS = 2
ATT_HEAD_DIM = 128
ATT_WIDTH = ATT_HEADS * ATT_HEAD_DIM
IDX_HEADS = 8
IDX_DIM = 64
TOPK_MAX = 256
Q_BLOCK = 128
KEY_CHUNK = 512
KEY_BITS = 32
PEER_HEADS = 8
N_KEYS = 128
PEER_TOPK = 16
PEER_QDIM = 256

LANES = 128
SUBLANES = 8
VMEM_LIMIT = 56 * 1024 * 1024
INT_MIN = np.int32(-2 ** 31)
NEG_BIG = -1e30

_NT = (((1,), (1,)), ((), ()))


def _params(sem):
    return pltpu.CompilerParams(dimension_semantics=sem, vmem_limit_bytes=VMEM_LIMIT)


def _rms(x, g):
    return x * lax.rsqrt(jnp.mean(x * x, axis=-1, keepdims=True) + EPS) * g


def _sigmoid(x):
    return 1.0 / (1.0 + jnp.exp(-x))


def _split3(a):
    a1 = a.astype(BF16)
    r1 = a - a1.astype(F32)
    a2 = r1.astype(BF16)
    a3 = (r1 - a2.astype(F32)).astype(BF16)
    return a1, a2, a3


def _dot(a, b):
    return jnp.dot(a, b, preferred_element_type=F32)


def _norm_mm_kernel(x_ref, g_ref, w_ref, o_ref):
    h = _rms(x_ref[...], g_ref[...]).astype(BF16)
    o_ref[...] = _dot(h, w_ref[...]).astype(o_ref.dtype)


def norm_matmul(x, g, w, out_dtype=F32, tm=512):
    m, k = x.shape
    n = w.shape[1]
    return pl.pallas_call(
        _norm_mm_kernel,
        grid=(m // tm,),
        in_specs=[pl.BlockSpec((tm, k), lambda i: (i, 0)),
                  pl.BlockSpec((1, k), lambda i: (0, 0)),
                  pl.BlockSpec((k, n), lambda i: (0, 0))],
        out_specs=pl.BlockSpec((tm, n), lambda i: (i, 0)),
        out_shape=jax.ShapeDtypeStruct((m, n), out_dtype),
        compiler_params=_params(("parallel",)),
        name="norm_matmul",
    )(x, g, w)


def _norm_mm_t_kernel(x_ref, g_ref, wt_ref, o_ref):
    h = _rms(x_ref[...], g_ref[...]).astype(BF16)
    o_ref[...] = lax.dot_general(wt_ref[...], h, _NT, preferred_element_type=F32)


def norm_matmul_t(x, g, wt, tm=512):
    m, k = x.shape
    n = wt.shape[0]
    return pl.pallas_call(
        _norm_mm_t_kernel,
        grid=(m // tm,),
        in_specs=[pl.BlockSpec((tm, k), lambda i: (i, 0)),
                  pl.BlockSpec((1, k), lambda i: (0, 0)),
                  pl.BlockSpec((n, k), lambda i: (0, 0))],
        out_specs=pl.BlockSpec((n, tm), lambda i: (0, i)),
        out_shape=jax.ShapeDtypeStruct((n, m), F32),
        compiler_params=_params(("parallel",)),
        name="norm_matmul_t",
    )(x, g, wt)


def _ssm_kernel(zx_ref, dt_ref, convw_ref, convb_ref, dtb_ref, alog_ref,
                dskip_ref, normg_ref, expand_ref, y_ref, carry_ref, state_ref):
    @pl.when(pl.program_id(1) == 0)
    def _():
        carry_ref[...] = jnp.zeros_like(carry_ref)
        state_ref[...] = jnp.zeros_like(state_ref)

    xbc = zx_ref[:, D_INNER:]
    ext = jnp.concatenate([carry_ref[...], xbc], axis=0)
    conv = convb_ref[...]
    for j in range(CONV_WIDTH):
        lo = SUBLANES - (CONV_WIDTH - 1) + j
        conv = conv + convw_ref[j:j + 1, :] * ext[lo:lo + CHUNK]
    carry_ref[...] = xbc[CHUNK - SUBLANES:]
    xc = conv * _sigmoid(conv)
    xs = xc[:, :D_INNER]
    gw = SSM_GROUPS * D_STATE
    bm = xc[:, D_INNER:D_INNER + gw]
    cm = xc[:, D_INNER + gw:]

    lane = lax.broadcasted_iota(I32, (CHUNK, LANES), 1)
    row = lax.broadcasted_iota(I32, (CHUNK, LANES), 0)
    head_lane = lane < SSM_HEADS
    dt = jax.nn.softplus(dt_ref[...] + dtb_ref[...])
    adt = jnp.where(head_lane, dt * (-jnp.exp(alog_ref[...])), 0.0)
    tril = jnp.where(row >= lane, 1.0, 0.0).astype(BF16)
    cs = sum(_dot(tril, part) for part in _split3(adt))
    cs_row = cs.T
    cs_last = cs[CHUNK - 1:CHUNK, :]
    ecs = jnp.exp(cs)
    decay = jnp.exp(cs_last - cs)

    expand = expand_ref[...]

    def widen(a):
        return sum(_dot(part, expand) for part in _split3(a))

    dt_w = widen(dt)
    ecs_w = widen(ecs)
    decay_w = widen(decay)
    xdt = xs * dt_w
    xdt_b = xdt.astype(BF16)
    xdec_b = (xdt * decay_w).astype(BF16)
    causal = row >= lane
    pair_lo = lane < SSM_HEAD_DIM

    gh = SSM_HEADS // SSM_GROUPS
    gcols = gh * SSM_HEAD_DIM
    ys = []
    for g in range(SSM_GROUPS):
        bg = bm[:, g * D_STATE:(g + 1) * D_STATE]
        cg = cm[:, g * D_STATE:(g + 1) * D_STATE].astype(BF16)
        cb = lax.dot_general(cg, bg.astype(BF16), _NT, preferred_element_type=F32)
        st = state_ref[g]
        y_off = _dot(cg, st.astype(BF16))
        new_st = _dot(bg.T.astype(BF16), xdec_b[:, g * gcols:(g + 1) * gcols])
        state_ref[g] = st * ecs_w[CHUNK - 1:CHUNK, g * gcols:(g + 1) * gcols] + new_st
        pieces = []
        for pr in range(gh // 2):
            outs = []
            for sub in range(2):
                h = g * gh + pr * 2 + sub
                lm = jnp.where(causal, jnp.exp(cs[:, h:h + 1] - cs_row[h:h + 1, :]), 0.0)
                mm = (cb * lm).astype(BF16)
                c0 = (h // 2) * LANES
                outs.append(_dot(mm, xdt_b[:, c0:c0 + LANES]))
            pieces.append(jnp.where(pair_lo, outs[0], outs[1]))
        y_diag = jnp.concatenate(pieces, axis=1)
        ys.append(y_diag + y_off * ecs_w[:, g * gcols:(g + 1) * gcols])
    y = jnp.concatenate(ys, axis=1) + dskip_ref[...] * xs
    z = zx_ref[:, :D_INNER]
    y = y * (z * _sigmoid(z))
    outs = []
    for g in range(SSM_GROUPS):
        yg = y[:, g * gcols:(g + 1) * gcols]
        outs.append(yg * lax.rsqrt(jnp.mean(yg * yg, axis=-1, keepdims=True) + EPS))
    y_ref[...] = (jnp.concatenate(outs, axis=1) * normg_ref[...]).astype(y_ref.dtype)


def ssm_branch(zx, gates_dt, conv_w, conv_b, dt_bias, a_log, d_skip, norm_g, batch, seq):
    t = batch * seq
    nchunk = seq // CHUNK
    pad = LANES - SSM_HEADS
    convw_t = jnp.transpose(conv_w)
    dtb = jnp.pad(dt_bias, (0, pad)).reshape(1, LANES)
    alog = jnp.pad(a_log, (0, pad)).reshape(1, LANES)
    dskip = jnp.repeat(d_skip, SSM_HEAD_DIM).reshape(1, D_INNER)
    expand = (np.arange(LANES)[:, None] == (np.arange(D_INNER)[None, :] // SSM_HEAD_DIM))
    expand = jnp.asarray(expand, dtype=BF16)
    tok = lambda w: pl.BlockSpec((CHUNK, w), lambda b, c: (b * nchunk + c, 0))
    const = lambda r, w: pl.BlockSpec((r, w), lambda b, c: (0, 0))
    dt_block = (gates_dt.shape[1] - LANES) // LANES
    return pl.pallas_call(
        _ssm_kernel,
        grid=(batch, nchunk),
        in_specs=[tok(D_INNER + CONV_DIM), pl.BlockSpec((CHUNK, LANES), lambda b, c: (b * nchunk + c, dt_block)),
                  const(CONV_WIDTH, CONV_DIM), const(1, CONV_DIM), const(1, LANES),
                  const(1, LANES), const(1, D_INNER), const(1, D_INNER),
                  const(LANES, D_INNER)],
        out_specs=tok(D_INNER),
        out_shape=jax.ShapeDtypeStruct((t, D_INNER), BF16),
        scratch_shapes=[pltpu.VMEM((SUBLANES, CONV_DIM), F32),
                        pltpu.VMEM((SSM_GROUPS, D_STATE, D_INNER // SSM_GROUPS), F32)],
        compiler_params=_params(("parallel", "arbitrary")),
        name="ssm_branch",
    )(zx, gates_dt, convw_t, conv_b.reshape(1, CONV_DIM), dtb, alog, dskip,
      norm_g.reshape(1, D_INNER), expand)


def _rope_kernel(pos_ref, inva_ref, invi_ref, ca_ref, sa_ref, ci_ref, si_ref):
    pos = pos_ref[...]
    ang_a = inva_ref[...] * pos
    ang_i = invi_ref[...] * pos
    ca_ref[...] = jnp.cos(ang_a)
    sa_ref[...] = jnp.sin(ang_a)
    ci_ref[...] = jnp.cos(ang_i)
    si_ref[...] = jnp.sin(ang_i)


def rope_tables_t(positions, tile=2048):
    t = positions.size
    tile = min(tile, t)
    pos = positions.reshape(1, t).astype(F32)
    ha, hi = ATT_HEAD_DIM // 2, IDX_DIM // 2
    inv_a = (1.0 / (ROPE_THETA ** (jnp.arange(0, ATT_HEAD_DIM, 2, dtype=F32) / ATT_HEAD_DIM))).reshape(ha, 1)
    inv_i = (1.0 / (ROPE_THETA ** (jnp.arange(0, IDX_DIM, 2, dtype=F32) / IDX_DIM))).reshape(hi, 1)
    col = lambda r: pl.BlockSpec((r, tile), lambda i: (0, i))
    return pl.pallas_call(
        _rope_kernel,
        grid=(t // tile,),
        in_specs=[col(1), pl.BlockSpec((ha, 1), lambda i: (0, 0)), pl.BlockSpec((hi, 1), lambda i: (0, 0))],
        out_specs=[col(ha), col(ha), col(hi), col(hi)],
        out_shape=[jax.ShapeDtypeStruct((ha, t), F32)] * 2 + [jax.ShapeDtypeStruct((hi, t), F32)] * 2,
        compiler_params=_params(("parallel",)),
        name="rope_tables",
    )(pos, inv_a, inv_i)


_Q0 = 0
_K0 = _Q0 + ATT_WIDTH
_V0 = _K0 + ATT_KV_HEADS * ATT_HEAD_DIM
_QI0 = _V0 + ATT_KV_HEADS * ATT_HEAD_DIM
_KI0 = _QI0 + IDX_HEADS * IDX_DIM
_WI0 = _KI0 + IDX_DIM
_ATT_ROWS = _WI0 + IDX_HEADS


def _rope_rows(x, cos, sin):
    half = x.shape[0] // 2
    x1, x2 = x[:half], x[half:]
    return jnp.concatenate([x1 * cos - x2 * sin, x1 * sin + x2 * cos], axis=0)


def _head_norm_rows(x, g):
    return x * lax.rsqrt(jnp.mean(x * x, axis=0, keepdims=True) + EPS) * g


def _dsa_prep_kernel(p_ref, ca_ref, sa_ref, ci_ref, si_ref, qg_ref, kg_ref,
                     qt_ref, k_ref, v_ref, qit_ref, ki_ref, wit_ref):
    tile = p_ref.shape[1]
    ca, sa, ci, si = ca_ref[...], sa_ref[...], ci_ref[...], si_ref[...]
    hd = ATT_HEAD_DIM
    for h in range(ATT_HEADS):
        x = _head_norm_rows(p_ref[_Q0 + h * hd:_Q0 + (h + 1) * hd, :], qg_ref[...])
        qt_ref[h * hd:(h + 1) * hd, :] = (_rope_rows(x, ca, sa) * (hd ** -0.5)).astype(BF16)
    for h in range(ATT_KV_HEADS):
        x = _head_norm_rows(p_ref[_K0 + h * hd:_K0 + (h + 1) * hd, :], kg_ref[...])
        k_ref[:, h * hd:(h + 1) * hd] = _rope_rows(x, ca, sa).T.astype(BF16)
    for s in range(tile // KEY_CHUNK):
        v_ref[s] = p_ref[_V0:_V0 + ATT_KV_HEADS * hd, s * KEY_CHUNK:(s + 1) * KEY_CHUNK].astype(BF16)
    for h in range(IDX_HEADS):
        x = _rope_rows(p_ref[_QI0 + h * IDX_DIM:_QI0 + (h + 1) * IDX_DIM, :], ci, si) * (IDX_DIM ** -0.5)
        x = x.astype(BF16)
        for s in range(tile // Q_BLOCK):
            qit_ref[s, :, h * Q_BLOCK:(h + 1) * Q_BLOCK] = x[:, s * Q_BLOCK:(s + 1) * Q_BLOCK]
    x = _rope_rows(p_ref[_KI0:_KI0 + IDX_DIM, :], ci, si)
    zpad = jnp.zeros((LANES - IDX_DIM, tile), F32)
    ki_ref[...] = jnp.concatenate([x, zpad], axis=0).T[:, :IDX_DIM].astype(BF16)
    wit_ref[...] = p_ref[_WI0:_WI0 + IDX_HEADS, :] * (IDX_HEADS ** -0.5)


def dsa_prep(proj_t, tables, q_norm, k_norm, tile=512):
    t = proj_t.shape[1]
    tile = min(tile, t)
    ca, sa, ci, si = tables
    ha, hi = ATT_HEAD_DIM // 2, IDX_DIM // 2
    col = lambda r: pl.BlockSpec((r, tile), lambda i: (0, i))
    rowb = lambda w: pl.BlockSpec((tile, w), lambda i: (i, 0))
    const = lambda r, w: pl.BlockSpec((r, w), lambda i: (0, 0))
    kvw = ATT_KV_HEADS * ATT_HEAD_DIM
    return pl.pallas_call(
        _dsa_prep_kernel,
        grid=(t // tile,),
        in_specs=[col(_ATT_ROWS), col(ha), col(ha), col(hi), col(hi),
                  const(ATT_HEAD_DIM, 1), const(ATT_HEAD_DIM, 1)],
        out_specs=[col(ATT_WIDTH), rowb(kvw),
                   pl.BlockSpec((tile // KEY_CHUNK, kvw, KEY_CHUNK), lambda i: (i, 0, 0)),
                   pl.BlockSpec((tile // Q_BLOCK, IDX_DIM, IDX_HEADS * Q_BLOCK), lambda i: (i, 0, 0)),
                   rowb(IDX_DIM), col(IDX_HEADS)],
        out_shape=[jax.ShapeDtypeStruct((ATT_WIDTH, t), BF16),
                   jax.ShapeDtypeStruct((t, kvw), BF16),
                   jax.ShapeDtypeStruct((t // KEY_CHUNK, kvw, KEY_CHUNK), BF16),
                   jax.ShapeDtypeStruct((t // Q_BLOCK, IDX_DIM, IDX_HEADS * Q_BLOCK), BF16),
                   jax.ShapeDtypeStruct((t, IDX_DIM), BF16),
                   jax.ShapeDtypeStruct((IDX_HEADS, t), F32)],
        compiler_params=_params(("parallel",)),
        name="dsa_prep",
    )(proj_t, ca, sa, ci, si, q_norm.reshape(ATT_HEAD_DIM, 1), k_norm.reshape(ATT_HEAD_DIM, 1))


def _bit_transpose32(words):
    a = list(words)
    for j, mask in ((16, 0x0000FFFF), (8, 0x00FF00FF), (4, 0x0F0F0F0F), (2, 0x33333333), (1, 0x55555555)):
        for k in range(32):
            if k & j == 0:
                t = (lax.shift_right_logical(a[k], np.int32(j)) ^ a[k + j]) & np.int32(mask)
                a[k + j] = a[k + j] ^ t
                a[k] = a[k] ^ lax.shift_left(t, np.int32(j))
    return a


def _dsa_kernel(ki_ref, k_ref, v_ref, qit_ref, qt_ref, wit_ref, o_ref, keys_ref, acc_ref, s_ref,
                planes_ref, *, topk, idx_bits):
    qb = Q_BLOCK
    kc = KEY_CHUNK
    j = pl.program_id(1)
    nk = (j * qb + qb + kc - 1) // kc
    row = lax.broadcasted_iota(I32, (kc, qb), 0)
    col = lax.broadcasted_iota(I32, (kc, qb), 1)
    q_pos = j * qb + col
    prows = kc // KEY_BITS

    def score_chunk(c, carry):
        ks = pl.multiple_of(c * kc, kc)
        s = _dot(ki_ref[pl.ds(ks, kc), :], qit_ref[0])
        acc = jnp.zeros((kc, qb), F32)
        for h in range(IDX_HEADS):
            acc = acc + wit_ref[h:h + 1, :] * jnp.maximum(s[:, h * qb:(h + 1) * qb], 0.0)
        acc = jnp.where(acc == 0.0, 0.0, acc)
        bits = pltpu.bitcast(acc, I32)
        key = jnp.where(bits < 0, bits ^ np.int32(0x7FFFFFFF), bits)
        key = jnp.where(ks + row <= q_pos, key, INT_MIN)
        keys_ref[pl.ds(ks, kc), :] = key
        u = key ^ INT_MIN
        for sub in range(prows // SUBLANES):
            r0 = sub * KEY_BITS * SUBLANES
            planes = _bit_transpose32([u[r0 + i * SUBLANES:r0 + (i + 1) * SUBLANES] for i in range(KEY_BITS)])
            ps = pl.multiple_of(c * prows + sub * SUBLANES, SUBLANES)
            for b in range(KEY_BITS):
                planes_ref[b, pl.ds(ps, SUBLANES), :] = planes[b]
        return carry

    lax.fori_loop(0, nk, score_chunk, 0)

    def clear_chunk(c, carry):
        ps = pl.multiple_of(c * prows, prows)
        for b in range(KEY_BITS):
            planes_ref[b, pl.ds(ps, prows), :] = jnp.zeros((prows, qb), I32)
        return carry

    lax.fori_loop(nk, planes_ref.shape[1] // prows, clear_chunk, 0)

    def count(hit):
        def body(c, acc):
            ks = pl.multiple_of(c * kc, kc)
            one = hit(keys_ref[pl.ds(ks, kc), :], ks + row)
            return acc + one.reshape(kc // SUBLANES, SUBLANES, qb).sum(axis=0)
        acc = lax.fori_loop(0, nk, body, jnp.zeros((SUBLANES, qb), I32))
        return acc.sum(axis=0, keepdims=True)

    nrows = planes_ref.shape[1]
    prow = lax.broadcasted_iota(I32, (nrows, qb), 0)
    active0 = jnp.where(prow < nk * prows, np.int32(-1), np.int32(0))

    def value_bit(i, carry):
        active, need, prefix = carry
        hit = active & planes_ref[KEY_BITS - 1 - i]
        n = jnp.sum(lax.population_count(hit), axis=0, keepdims=True)
        take = n >= need
        active = jnp.where(take, hit, active ^ hit)
        need = jnp.where(take, need, need - n)
        prefix = jnp.where(take, prefix | jnp.left_shift(np.int32(1), KEY_BITS - 1 - i), prefix)
        return active, need, prefix

    active, need, prefix = lax.fori_loop(
        0, KEY_BITS, value_bit, (active0, jnp.full((1, qb), topk, I32), jnp.zeros((1, qb), I32)))
    thr = prefix ^ INT_MIN

    n_tied = jnp.sum(lax.population_count(active), axis=0, keepdims=True)
    excess = jnp.max(jnp.where(thr > INT_MIN, n_tied - need, 0))

    def tie_search():
        def index_bit(i, prefix):
            cand = prefix | jnp.left_shift(np.int32(1), idx_bits - 1 - i)
            n = count(lambda key, idx: jnp.where(key == thr, jnp.where(idx < cand, 1, 0), 0))
            return jnp.where(n < need, cand, prefix)
        return lax.fori_loop(0, idx_bits, index_bit, jnp.zeros((1, qb), I32))

    last = lax.cond(excess > 0, tie_search, lambda: jnp.full((1, qb), 2 ** idx_bits, I32))

    rep = ATT_HEADS // ATT_KV_HEADS
    hd = ATT_HEAD_DIM
    wide = rep * qb
    groups = range(ATT_KV_HEADS)
    qgs = [jnp.concatenate([qt_ref[(g * rep + r) * hd:(g * rep + r + 1) * hd, :] for r in range(rep)], axis=1)
           for g in groups]

    def fold(x, op):
        return op(x.reshape(kc // SUBLANES, SUBLANES, wide), axis=0)

    def score_sweep(c, carry):
        ks = pl.multiple_of(c * kc, kc)
        key = keys_ref[pl.ds(ks, kc), :]
        idx = ks + row
        tied = jnp.where(key == thr, jnp.where(idx <= last, 0.0, NEG_BIG), NEG_BIG)
        b = jnp.where(idx <= q_pos, jnp.where(key > thr, 0.0, tied), NEG_BIG)
        b = jnp.concatenate([b] * rep, axis=1)
        out = []
        for g in groups:
            s = _dot(k_ref[pl.ds(ks, kc), g * hd:(g + 1) * hd], qgs[g]) + b
            s_ref[g, pl.ds(ks, kc), :] = s
            out.append(jnp.maximum(carry[g], fold(s, jnp.max)))
        return tuple(out)

    tops = lax.fori_loop(0, nk, score_sweep, tuple(jnp.full((SUBLANES, wide), NEG_BIG, F32) for _ in groups))
    tops = [jnp.max(m, axis=0, keepdims=True) for m in tops]
    acc_ref[...] = jnp.zeros_like(acc_ref)

    def value_sweep(c, carry):
        ks = pl.multiple_of(c * kc, kc)
        out = []
        for g in groups:
            p = jnp.exp(s_ref[g, pl.ds(ks, kc), :] - tops[g])
            acc_ref[g] += _dot(v_ref[c, g * hd:(g + 1) * hd, :], p.astype(BF16))
            out.append(carry[g] + fold(p, jnp.sum))
        return tuple(out)

    sums = lax.fori_loop(0, nk, value_sweep, tuple(jnp.zeros((SUBLANES, wide), F32) for _ in groups))
    for g in groups:
        o = acc_ref[g] / jnp.sum(sums[g], axis=0, keepdims=True)
        for r in range(rep):
            h = g * rep + r
            o_ref[:, h * hd:(h + 1) * hd] = o[:, r * qb:(r + 1) * qb].T.astype(o_ref.dtype)


def dsa_attention(ki, k, v3, qit, qt, wit, batch, seq):
    t = batch * seq
    nqb = seq // Q_BLOCK
    kvw = ATT_KV_HEADS * ATT_HEAD_DIM
    topk = min(TOPK_MAX, seq // 4)
    idx_bits = max(1, int(math.ceil(math.log2(seq))))
    assert seq % KEY_CHUNK == 0 and KEY_CHUNK % (KEY_BITS * SUBLANES) == 0 and KEY_CHUNK % Q_BLOCK == 0
    qcol = lambda r: pl.BlockSpec((r, Q_BLOCK), lambda b, j: (0, b * nqb + j))
    return pl.pallas_call(
        functools.partial(_dsa_kernel, topk=topk, idx_bits=idx_bits),
        grid=(batch, nqb),
        in_specs=[pl.BlockSpec((seq, IDX_DIM), lambda b, j: (b, 0)),
                  pl.BlockSpec((seq, kvw), lambda b, j: (b, 0)),
                  pl.BlockSpec((seq // KEY_CHUNK, kvw, KEY_CHUNK), lambda b, j: (b, 0, 0)),
                  pl.BlockSpec((1, IDX_DIM, IDX_HEADS * Q_BLOCK), lambda b, j: (b * nqb + j, 0, 0)),
                  qcol(ATT_WIDTH), qcol(IDX_HEADS)],
        out_specs=pl.BlockSpec((Q_BLOCK, ATT_WIDTH), lambda b, j: (b * nqb + j, 0)),
        out_shape=jax.ShapeDtypeStruct((t, ATT_WIDTH), BF16),
        scratch_shapes=[pltpu.VMEM((seq, Q_BLOCK), I32),
                        pltpu.VMEM((ATT_KV_HEADS, ATT_HEAD_DIM, (ATT_HEADS // ATT_KV_HEADS) * Q_BLOCK), F32),
                        pltpu.VMEM((ATT_KV_HEADS, seq, (ATT_HEADS // ATT_KV_HEADS) * Q_BLOCK), F32),
                        pltpu.VMEM((KEY_BITS, seq // KEY_BITS, Q_BLOCK), I32)],
        compiler_params=_params(("parallel", "arbitrary")),
        name="dsa_attention",
    )(ki, k, v3, qit, qt, wit)


def _merge_kernel(x_ref, ys_ref, ya_ref, gate_ref, wbs_ref, wba_ref, wo_ref, o_ref):
    d = x_ref.shape[1]
    gates = gate_ref[...]
    merged = (_sigmoid(gates[:, :d]) * _dot(ys_ref[...], wbs_ref[...])
              + _sigmoid(gates[:, d:]) * _dot(ya_ref[...], wba_ref[...]))
    o_ref[...] = x_ref[...] + _dot(merged.astype(BF16), wo_ref[...])


def merge_branches(x, y_ssm, y_att, gates, w_bs, w_ba, w_o, tm=512):
    t, d = x.shape
    tm = min(tm, t)
    tok = lambda w: pl.BlockSpec((tm, w), lambda i: (i, 0))
    const = lambda r, w: pl.BlockSpec((r, w), lambda i: (0, 0))
    return pl.pallas_call(
        _merge_kernel,
        grid=(t // tm,),
        in_specs=[tok(d), tok(D_INNER), tok(ATT_WIDTH), tok(2 * d),
                  const(D_INNER, d), const(ATT_WIDTH, d), const(d, d)],
        out_specs=tok(d),
        out_shape=jax.ShapeDtypeStruct((t, d), F32),
        compiler_params=_params(("parallel",)),
        name="merge_branches",
    )(x, y_ssm, y_att, gates, w_bs, w_ba, w_o)


def _top_rows(s, k, rows_out):
    rows = lax.broadcasted_iota(I32, (rows_out, s.shape[1]), 0)

    def body(r, carry):
        s, top = carry
        m = jnp.max(s, axis=0, keepdims=True)
        return jnp.where(s == m, -jnp.inf, s), jnp.where(rows == r, m, top)

    _, top = lax.fori_loop(0, k, body, (s, jnp.full((rows_out, s.shape[1]), -jnp.inf, F32)))
    return top


def _peer_route_kernel(x_ref, g_ref, wqt_ref, keys_ref, ht_ref, tau_ref, e1_ref, e2_ref):
    tm = x_ref.shape[0]
    h = _rms(x_ref[...], g_ref[...])
    ht_ref[...] = h.T.astype(BF16)
    qt = lax.dot_general(wqt_ref[...], h.astype(BF16), _NT, preferred_element_type=F32).astype(BF16)
    half = PEER_QDIM // 2
    k = PEER_TOPK
    k1 = k + 1
    rows_out = -(-k1 // SUBLANES) * SUBLANES
    sub8 = lax.broadcasted_iota(I32, (SUBLANES, tm), 0)
    for hh in range(PEER_HEADS):
        s1 = _dot(keys_ref[2 * hh], qt[(2 * hh) * half:(2 * hh + 1) * half, :])
        s2 = _dot(keys_ref[2 * hh + 1], qt[(2 * hh + 1) * half:(2 * hh + 2) * half, :])
        a = _top_rows(s1, k1, rows_out)
        b = _top_rows(s2, k1, rows_out)
        assert k1 // (SUBLANES + 1) <= 1
        tiles = [a[r:r + SUBLANES] + b[0:1] for r in range(SUBLANES, rows_out, SUBLANES)]
        for i in range(SUBLANES):
            lim = k1 // (i + 1)
            for r in range(0, lim, SUBLANES):
                tile = b[r:r + SUBLANES] + a[i:i + 1]
                tiles.append(tile if lim - r >= SUBLANES else jnp.where(sub8 < lim - r, tile, -jnp.inf))
        cand = jnp.concatenate(tiles, axis=0)
        best = a[0:1] + b[0:1]

        def body(r, carry):
            cand, zsum, kth, nxt = carry
            m = jnp.max(cand, axis=0, keepdims=True)
            zsum = zsum + jnp.where(r < k, jnp.exp(m - best), 0.0)
            return jnp.where(cand == m, -jnp.inf, cand), zsum, jnp.where(r == k - 1, m, kth), m

        _, zsum, kth, nxt = lax.fori_loop(0, k1, body, (cand, jnp.zeros((1, tm), F32), best, best))
        cut = 0.5 * (kth + nxt)
        r0 = hh * N_KEYS
        tau_ref[r0:r0 + N_KEYS, :] = jnp.exp((cut - b[0:1]) - s1)
        e1_ref[r0:r0 + N_KEYS, :] = jnp.exp(s1 - a[0:1]) * (0.5 / zsum)
        e2_ref[r0:r0 + N_KEYS, :] = jnp.exp(s2 - b[0:1])


def peer_route(x, g, wq_t, keys, tm=256):
    t, d = x.shape
    tm = min(tm, t)
    rows = PEER_HEADS * N_KEYS
    col = lambda r: pl.BlockSpec((r, tm), lambda i: (0, i))
    return pl.pallas_call(
        _peer_route_kernel,
        grid=(t // tm,),
        in_specs=[pl.BlockSpec((tm, d), lambda i: (i, 0)),
                  pl.BlockSpec((1, d), lambda i: (0, 0)),
                  pl.BlockSpec(wq_t.shape, lambda i: (0, 0)),
                  pl.BlockSpec(keys.shape, lambda i: (0, 0, 0))],
        out_specs=[col(d), col(rows), col(rows), col(rows)],
        out_shape=[jax.ShapeDtypeStruct((d, t), BF16)] + [jax.ShapeDtypeStruct((rows, t), F32)] * 3,
        compiler_params=_params(("parallel",)),
        name="peer_route",
    )(x, g, wq_t, keys)


PEER_ROWS = 32


def _peer_dense_kernel(x_ref, ht_ref, tau_ref, e1_ref, e2_ref, u_ref, vt_ref, o_ref, acc_ref, hid_ref, act_ref,
                       *, ib, nblk):
    j = pl.program_id(1)
    tt = ht_ref.shape[1]
    cur = j % 2
    prev = 1 - cur

    @pl.when(j == 0)
    def _():
        acc_ref[...] = jnp.zeros_like(acc_ref)

    def second_matmul():
        acc_ref[...] += _dot(vt_ref[...], act_ref[cur])

    def first_matmul():
        hid_ref[cur] = _dot(u_ref[...], ht_ref[...])

    def gate_stage():
        blk = j - 1
        for ii in range(ib):
            i1 = blk * ib + ii
            taus = [tau_ref[pl.ds(hh * N_KEYS + i1, 1), :] for hh in range(PEER_HEADS)]
            e1s = [e1_ref[pl.ds(hh * N_KEYS + i1, 1), :] for hh in range(PEER_HEADS)]
            for lc in range(tt // LANES):
                cols = slice(lc * LANES, (lc + 1) * LANES)
                tau_b = [jnp.broadcast_to(taus[hh][:, cols], (PEER_ROWS, LANES)) for hh in range(PEER_HEADS)]
                e1_b = [jnp.broadcast_to(e1s[hh][:, cols], (PEER_ROWS, LANES)) for hh in range(PEER_HEADS)]
                for rb in range(N_KEYS // PEER_ROWS):
                    gate = jnp.zeros((PEER_ROWS, LANES), F32)
                    for hh in range(PEER_HEADS):
                        r0 = hh * N_KEYS + rb * PEER_ROWS
                        e2 = e2_ref[r0:r0 + PEER_ROWS, cols]
                        gate = gate + jnp.where(e2 >= tau_b[hh], e2, 0.0) * e1_b[hh]
                    rows = slice(ii * N_KEYS + rb * PEER_ROWS, ii * N_KEYS + (rb + 1) * PEER_ROWS)
                    hb = hid_ref[prev, rows, cols]
                    act = hb * (1.0 + lax.erf(hb * np.float32(math.sqrt(0.5))))
                    act_ref[prev, rows, cols] = (act * gate).astype(BF16)

    interior = jnp.logical_and(j >= 2, j < nblk)

    @pl.when(interior)
    def _():
        second_matmul()
        gate_stage()
        first_matmul()

    @pl.when(jnp.logical_and(jnp.logical_not(interior), j >= 2))
    def _():
        second_matmul()

    @pl.when(jnp.logical_and(jnp.logical_not(interior), jnp.logical_and(j >= 1, j <= nblk)))
    def _():
        gate_stage()

    @pl.when(jnp.logical_and(jnp.logical_not(interior), j < nblk))
    def _():
        first_matmul()

    @pl.when(j == pl.num_programs(1) - 1)
    def _():
        o_ref[...] = x_ref[...] + acc_ref[...].T


def peer_dense(x, h_t, tau, e1, e2, u, v_t, tt=512, ib=4):
    t, d = x.shape
    tt = min(tt, t)
    rows = PEER_HEADS * N_KEYS
    nblk = N_KEYS // ib
    col = lambda r: pl.BlockSpec((r, tt), lambda i, j: (0, i))
    tok = pl.BlockSpec((tt, d), lambda i, j: (i, 0))
    return pl.pallas_call(
        functools.partial(_peer_dense_kernel, ib=ib, nblk=nblk),
        grid=(t // tt, nblk + 2),
        in_specs=[tok, col(d), col(rows), col(rows), col(rows),
                  pl.BlockSpec((ib * N_KEYS, d), lambda i, j: (jnp.minimum(j, nblk - 1), 0)),
                  pl.BlockSpec((d, ib * N_KEYS), lambda i, j: (0, jnp.clip(j - 2, 0, nblk - 1)))],
        out_specs=tok,
        out_shape=jax.ShapeDtypeStruct((t, d), F32),
        scratch_shapes=[pltpu.VMEM((d, tt), F32), pltpu.VMEM((2, ib * N_KEYS, tt), F32),
                        pltpu.VMEM((2, ib * N_KEYS, tt), BF16)],
        compiler_params=_params(("parallel", "arbitrary")),
        name="peer_dense",
    )(x, h_t, tau, e1, e2, u, v_t)


def _ple_kernel(x_ref, p_ref, g_ref, wg_ref, wp_ref, o_ref):
    x = x_ref[...]
    h = _rms(x, g_ref[...]).astype(BF16)
    gate = _sigmoid(_dot(h, wg_ref[...]))
    o_ref[...] = x + gate * _dot(p_ref[...].astype(BF16), wp_ref[...])


def ple_block(x, p, g, w_gate, w_proj, tm=512):
    t, d = x.shape
    tm = min(tm, t)
    pd = p.shape[1]
    return pl.pallas_call(
        _ple_kernel,
        grid=(t // tm,),
        in_specs=[pl.BlockSpec((tm, d), lambda i: (i, 0)),
                  pl.BlockSpec((tm, pd), lambda i: (i, 0)),
                  pl.BlockSpec((1, d), lambda i: (0, 0)),
                  pl.BlockSpec((d, d), lambda i: (0, 0)),
                  pl.BlockSpec((pd, d), lambda i: (0, 0))],
        out_specs=pl.BlockSpec((tm, d), lambda i: (i, 0)),
        out_shape=jax.ShapeDtypeStruct((t, d), F32),
        compiler_params=_params(("parallel",)),
        name="ple_block",
    )(x, p, g, w_gate, w_proj)


def _split_w_in(w):
    sizes = [D_INNER, CONV_DIM, SSM_HEADS, ATT_WIDTH, ATT_KV_HEADS * ATT_HEAD_DIM,
             ATT_KV_HEADS * ATT_HEAD_DIM, IDX_HEADS * IDX_DIM, IDX_DIM, IDX_HEADS, 2 * w.shape[0]]
    out, acc = [], 0
    for s in sizes:
        out.append(w[:, acc:acc + s])
        acc += s
    return out


def kernel(x, p, positions, norm_mix, w_in, conv_w, conv_b, dt_bias, a_log, d_skip, ssm_norm, q_norm, k_norm,
           w_branch, w_out, norm_ffn, peer_wq, peer_keys, peer_u, peer_v, norm_ple, w_ple_gate, w_ple_proj):
    batch, seq, d = x.shape
    t = batch * seq
    depth = w_in.shape[0]
    tables = rope_tables_t(positions)
    xf = x.reshape(t, d)
    for i in range(depth):
        wz, wxbc, wdt, wq, wk, wv, wqi, wki, wwi, wg = _split_w_in(w_in[i])
        w_dt = jnp.pad(wdt, ((0, 0), (0, LANES - SSM_HEADS))).astype(BF16)
        w_att_t = jnp.transpose(jnp.concatenate([wq, wk, wv, wqi, wki, wwi], axis=1)).astype(BF16)
        g_mix = norm_mix[i].reshape(1, d)

        zx = norm_matmul(xf, g_mix, jnp.concatenate([wz, wxbc], axis=1).astype(BF16))
        gates_dt = norm_matmul(xf, g_mix, jnp.concatenate([wg.astype(BF16), w_dt], axis=1))
        proj_t = norm_matmul_t(xf, g_mix, w_att_t)

        y_ssm = ssm_branch(zx, gates_dt, conv_w[i], conv_b[i], dt_bias[i], a_log[i], d_skip[i],
                           ssm_norm[i], batch, seq)
        qt, k, v3, qit, ki, wit = dsa_prep(proj_t, tables, q_norm[i], k_norm[i])
        y_att = dsa_attention(ki, k, v3, qit, qt, wit, batch, seq)
        xf = merge_branches(xf, y_ssm, y_att, gates_dt, w_branch[i, :D_INNER].astype(BF16),
                            w_branch[i, D_INNER:].astype(BF16), w_out[i].astype(BF16))

        keys = peer_keys[i].reshape(PEER_HEADS * 2, N_KEYS, PEER_QDIM // 2).astype(BF16)
        h_t, tau, e1, e2 = peer_route(xf, norm_ffn[i].reshape(1, d), jnp.transpose(peer_wq[i]).astype(BF16), keys)
        xf = peer_dense(xf, h_t, tau, e1, e2, peer_u[i].astype(BF16), jnp.transpose(peer_v[i]).astype(BF16))

        xf = ple_block(xf, p[i].reshape(t, -1), norm_ple[i].reshape(1, d),
                       w_ple_gate[i].astype(BF16), w_ple_proj[i].astype(BF16))
    return xf.reshape(batch, seq, d)
```

```python
import functools
import math

import numpy as np
import jax
import jax.numpy as jnp
from jax import lax
from jax.experimental import pallas as pl
from jax.experimental.pallas import tpu as pltpu

F32 = jnp.float32
BF16 = jnp.bfloat16
I32 = jnp.int32

EPS = 1e-6
ROPE_THETA = 10000.0

SSM_HEADS = 32
SSM_HEAD_DIM = 64
D_INNER = SSM_HEADS * SSM_HEAD_DIM
SSM_GROUPS = 4
D_STATE = 128
CONV_WIDTH = 4
CHUNK = 128
CONV_DIM = D_INNER + 2 * SSM_GROUPS * D_STATE
ATT_HEADS = 8
ATT_KV_HEADS = 2
ATT_HEAD_DIM = 128
ATT_WIDTH = ATT_HEADS * ATT_HEAD_DIM
IDX_HEADS = 8
IDX_DIM = 64
TOPK_MAX = 256
Q_BLOCK = 128
KEY_CHUNK = 512
KEY_BITS = 32
PEER_HEADS = 8
N_KEYS = 128
PEER_TOPK = 16
PEER_QDIM = 256

LANES = 128
SUBLANES = 8
VMEM_LIMIT = 56 * 1024 * 1024
INT_MIN = np.int32(-2 ** 31)
NEG_BIG = -1e30

_NT = (((1,), (1,)), ((), ()))


def _params(sem):
    return pltpu.CompilerParams(dimension_semantics=sem, vmem_limit_bytes=VMEM_LIMIT)


def _rms(x, g):
    return x * lax.rsqrt(jnp.mean(x * x, axis=-1, keepdims=True) + EPS) * g


def _sigmoid(x):
    return 1.0 / (1.0 + jnp.exp(-x))


def _split3(a):
    a1 = a.astype(BF16)
    r1 = a - a1.astype(F32)
    a2 = r1.astype(BF16)
    a3 = (r1 - a2.astype(F32)).astype(BF16)
    return a1, a2, a3


def _dot(a, b):
    return jnp.dot(a, b, preferred_element_type=F32)


def _norm_mm_kernel(x_ref, g_ref, w_ref, o_ref):
    h = _rms(x_ref[...], g_ref[...]).astype(BF16)
    o_ref[...] = _dot(h, w_ref[...]).astype(o_ref.dtype)


def norm_matmul(x, g, w, out_dtype=F32, tm=512):
    m, k = x.shape
    n = w.shape[1]
    return pl.pallas_call(
        _norm_mm_kernel,
        grid=(m // tm,),
        in_specs=[pl.BlockSpec((tm, k), lambda i: (i, 0)),
                  pl.BlockSpec((1, k), lambda i: (0, 0)),
                  pl.BlockSpec((k, n), lambda i: (0, 0))],
        out_specs=pl.BlockSpec((tm, n), lambda i: (i, 0)),
        out_shape=jax.ShapeDtypeStruct((m, n), out_dtype),
        compiler_params=_params(("parallel",)),
        name="norm_matmul",
    )(x, g, w)


def _norm_mm_t_kernel(x_ref, g_ref, wt_ref, o_ref):
    h = _rms(x_ref[...], g_ref[...]).astype(BF16)
    o_ref[0] = lax.dot_general(wt_ref[...], h, _NT, preferred_element_type=F32)


def norm_matmul_t(x, g, wt, tm=512):
    m, k = x.shape
    n = wt.shape[0]
    return pl.pallas_call(
        _norm_mm_t_kernel,
        grid=(m // tm,),
        in_specs=[pl.BlockSpec((tm, k), lambda i: (i, 0)),
                  pl.BlockSpec((1, k), lambda i: (0, 0)),
                  pl.BlockSpec((n, k), lambda i: (0, 0))],
        out_specs=pl.BlockSpec((1, n, tm), lambda i: (i, 0, 0)),
        out_shape=jax.ShapeDtypeStruct((m // tm, n, tm), F32),
        compiler_params=_params(("parallel",)),
        name="norm_matmul_t",
    )(x, g, wt)


def _ssm_kernel(zx_ref, dt_ref, convw_ref, convb_ref, dtb_ref, alog_ref,
                dskip_ref, normg_ref, expand_ref, y_ref, carry_ref, state_ref):
    @pl.when(pl.program_id(1) == 0)
    def _():
        carry_ref[...] = jnp.zeros_like(carry_ref)
        state_ref[...] = jnp.zeros_like(state_ref)

    xbc = zx_ref[:, D_INNER:]
    ext = jnp.concatenate([carry_ref[...], xbc], axis=0)
    conv = convb_ref[...]
    for j in range(CONV_WIDTH):
        lo = SUBLANES - (CONV_WIDTH - 1) + j
        conv = conv + convw_ref[j:j + 1, :] * ext[lo:lo + CHUNK]
    carry_ref[...] = xbc[CHUNK - SUBLANES:]
    xc = conv * _sigmoid(conv)
    xs = xc[:, :D_INNER]
    gw = SSM_GROUPS * D_STATE
    bm = xc[:, D_INNER:D_INNER + gw]
    cm = xc[:, D_INNER + gw:]

    lane = lax.broadcasted_iota(I32, (CHUNK, LANES), 1)
    row = lax.broadcasted_iota(I32, (CHUNK, LANES), 0)
    head_lane = lane < SSM_HEADS
    dt = jax.nn.softplus(dt_ref[...] + dtb_ref[...])
    adt = jnp.where(head_lane, dt * (-jnp.exp(alog_ref[...])), 0.0)
    tril = jnp.where(row >= lane, 1.0, 0.0).astype(BF16)
    cs = sum(_dot(tril, part) for part in _split3(adt))
    cs_row = cs.T
    cs_last = cs[CHUNK - 1:CHUNK, :]
    ecs = jnp.exp(cs)
    decay = jnp.exp(cs_last - cs)

    expand = expand_ref[...]

    def widen(a):
        return sum(_dot(part, expand) for part in _split3(a))

    dt_w = widen(dt)
    ecs_w = widen(ecs)
    decay_w = widen(decay)
    xdt = xs * dt_w
    xdt_b = xdt.astype(BF16)
    xdec_b = (xdt * decay_w).astype(BF16)
    causal = row >= lane
    pair_lo = lane < SSM_HEAD_DIM

    gh = SSM_HEADS // SSM_GROUPS
    gcols = gh * SSM_HEAD_DIM
    ys = []
    for g in range(SSM_GROUPS):
        bg = bm[:, g * D_STATE:(g + 1) * D_STATE]
        cg = cm[:, g * D_STATE:(g + 1) * D_STATE].astype(BF16)
        cb = lax.dot_general(cg, bg.astype(BF16), _NT, preferred_element_type=F32)
        st = state_ref[g]
        y_off = _dot(cg, st.astype(BF16))
        new_st = _dot(bg.T.astype(BF16), xdec_b[:, g * gcols:(g + 1) * gcols])
        state_ref[g] = st * ecs_w[CHUNK - 1:CHUNK, g * gcols:(g + 1) * gcols] + new_st
        pieces = []
        for pr in range(gh // 2):
            outs = []
            for sub in range(2):
                h = g * gh + pr * 2 + sub
                lm = jnp.where(causal, jnp.exp(cs[:, h:h + 1] - cs_row[h:h + 1, :]), 0.0)
                mm = (cb * lm).astype(BF16)
                c0 = (h // 2) * LANES
                outs.append(_dot(mm, xdt_b[:, c0:c0 + LANES]))
            pieces.append(jnp.where(pair_lo, outs[0], outs[1]))
        y_diag = jnp.concatenate(pieces, axis=1)
        ys.append(y_diag + y_off * ecs_w[:, g * gcols:(g + 1) * gcols])
    y = jnp.concatenate(ys, axis=1) + dskip_ref[...] * xs
    z = zx_ref[:, :D_INNER]
    y = y * (z * _sigmoid(z))
    outs = []
    for g in range(SSM_GROUPS):
        yg = y[:, g * gcols:(g + 1) * gcols]
        outs.append(yg * lax.rsqrt(jnp.mean(yg * yg, axis=-1, keepdims=True) + EPS))
    y_ref[...] = (jnp.concatenate(outs, axis=1) * normg_ref[...]).astype(y_ref.dtype)


def ssm_branch(zx, gates_dt, conv_w, conv_b, dt_bias, a_log, d_skip, norm_g, batch, seq):
    t = batch * seq
    nchunk = seq // CHUNK
    pad = LANES - SSM_HEADS
    convw_t = jnp.transpose(conv_w)
    dtb = jnp.pad(dt_bias, (0, pad)).reshape(1, LANES)
    alog = jnp.pad(a_log, (0, pad)).reshape(1, LANES)
    dskip = jnp.repeat(d_skip, SSM_HEAD_DIM).reshape(1, D_INNER)
    expand = (np.arange(LANES)[:, None] == (np.arange(D_INNER)[None, :] // SSM_HEAD_DIM))
    expand = jnp.asarray(expand, dtype=BF16)
    tok = lambda w: pl.BlockSpec((CHUNK, w), lambda b, c: (b * nchunk + c, 0))
    const = lambda r, w: pl.BlockSpec((r, w), lambda b, c: (0, 0))
    dt_block = (gates_dt.shape[1] - LANES) // LANES
    return pl.pallas_call(
        _ssm_kernel,
        grid=(batch, nchunk),
        in_specs=[tok(D_INNER + CONV_DIM), pl.BlockSpec((CHUNK, LANES), lambda b, c: (b * nchunk + c, dt_block)),
                  const(CONV_WIDTH, CONV_DIM), const(1, CONV_DIM), const(1, LANES),
                  const(1, LANES), const(1, D_INNER), const(1, D_INNER),
                  const(LANES, D_INNER)],
        out_specs=tok(D_INNER),
        out_shape=jax.ShapeDtypeStruct((t, D_INNER), BF16),
        scratch_shapes=[pltpu.VMEM((SUBLANES, CONV_DIM), F32),
                        pltpu.VMEM((SSM_GROUPS, D_STATE, D_INNER // SSM_GROUPS), F32)],
        compiler_params=_params(("parallel", "arbitrary")),
        name="ssm_branch",
    )(zx, gates_dt, convw_t, conv_b.reshape(1, CONV_DIM), dtb, alog, dskip,
      norm_g.reshape(1, D_INNER), expand)


def _rope_kernel(pos_ref, inva_ref, invi_ref, ca_ref, sa_ref, ci_ref, si_ref):
    pos = pos_ref[...]
    ang_a = inva_ref[...] * pos
    ang_i = invi_ref[...] * pos
    ca_ref[...] = jnp.cos(ang_a)
    sa_ref[...] = jnp.sin(ang_a)
    ci_ref[...] = jnp.cos(ang_i)
    si_ref[...] = jnp.sin(ang_i)


def rope_tables_t(positions, tile=2048):
    t = positions.size
    tile = min(tile, t)
    pos = positions.reshape(1, t).astype(F32)
    ha, hi = ATT_HEAD_DIM // 2, IDX_DIM // 2
    inv_a = (1.0 / (ROPE_THETA ** (jnp.arange(0, ATT_HEAD_DIM, 2, dtype=F32) / ATT_HEAD_DIM))).reshape(ha, 1)
    inv_i = (1.0 / (ROPE_THETA ** (jnp.arange(0, IDX_DIM, 2, dtype=F32) / IDX_DIM))).reshape(hi, 1)
    col = lambda r: pl.BlockSpec((r, tile), lambda i: (0, i))
    return pl.pallas_call(
        _rope_kernel,
        grid=(t // tile,),
        in_specs=[col(1), pl.BlockSpec((ha, 1), lambda i: (0, 0)), pl.BlockSpec((hi, 1), lambda i: (0, 0))],
        out_specs=[col(ha), col(ha), col(hi), col(hi)],
        out_shape=[jax.ShapeDtypeStruct((ha, t), F32)] * 2 + [jax.ShapeDtypeStruct((hi, t), F32)] * 2,
        compiler_params=_params(("parallel",)),
        name="rope_tables",
    )(pos, inv_a, inv_i)


_Q0 = 0
_K0 = _Q0 + ATT_WIDTH
_V0 = _K0 + ATT_KV_HEADS * ATT_HEAD_DIM
_QI0 = _V0 + ATT_KV_HEADS * ATT_HEAD_DIM
_KI0 = _QI0 + IDX_HEADS * IDX_DIM
_WI0 = _KI0 + IDX_DIM
_ATT_ROWS = _WI0 + IDX_HEADS


def _rope_rows(x, cos, sin):
    half = x.shape[0] // 2
    x1, x2 = x[:half], x[half:]
    return jnp.concatenate([x1 * cos - x2 * sin, x1 * sin + x2 * cos], axis=0)


def _head_norm_rows(x, g):
    return x * lax.rsqrt(jnp.mean(x * x, axis=0, keepdims=True) + EPS) * g


def _dsa_prep_kernel(p_ref, ca_ref, sa_ref, ci_ref, si_ref, qg_ref, kg_ref,
                     qt_ref, k_ref, v_ref, qit_ref, ki_ref, wit_ref):
    tile = p_ref.shape[2]
    ca, sa, ci, si = ca_ref[...], sa_ref[...], ci_ref[...], si_ref[...]
    hd = ATT_HEAD_DIM
    for h in range(ATT_HEADS):
        x = _head_norm_rows(p_ref[0, _Q0 + h * hd:_Q0 + (h + 1) * hd, :], qg_ref[...])
        x = (_rope_rows(x, ca, sa) * (hd ** -0.5)).astype(BF16)
        for s in range(tile // Q_BLOCK):
            qt_ref[s, h * hd:(h + 1) * hd, :] = x[:, s * Q_BLOCK:(s + 1) * Q_BLOCK]
    for h in range(ATT_KV_HEADS):
        x = _head_norm_rows(p_ref[0, _K0 + h * hd:_K0 + (h + 1) * hd, :], kg_ref[...])
        k_ref[:, h * hd:(h + 1) * hd] = _rope_rows(x, ca, sa).T.astype(BF16)
    for s in range(tile // KEY_CHUNK):
        v_ref[s] = p_ref[0, _V0:_V0 + ATT_KV_HEADS * hd, s * KEY_CHUNK:(s + 1) * KEY_CHUNK].astype(BF16)
    for h in range(IDX_HEADS):
        x = _rope_rows(p_ref[0, _QI0 + h * IDX_DIM:_QI0 + (h + 1) * IDX_DIM, :], ci, si) * (IDX_DIM ** -0.5)
        x = x.astype(BF16)
        for s in range(tile // Q_BLOCK):
            qit_ref[s, :, h * Q_BLOCK:(h + 1) * Q_BLOCK] = x[:, s * Q_BLOCK:(s + 1) * Q_BLOCK]
    x = _rope_rows(p_ref[0, _KI0:_KI0 + IDX_DIM, :], ci, si)
    zpad = jnp.zeros((LANES - IDX_DIM, tile), F32)
    ki_ref[...] = jnp.concatenate([x, zpad], axis=0).T[:, :IDX_DIM].astype(BF16)
    wit_ref[...] = p_ref[0, _WI0:_WI0 + IDX_HEADS, :] * (IDX_HEADS ** -0.5)


def dsa_prep(proj_t, tables, q_norm, k_norm):
    nt, _, tile = proj_t.shape
    t = nt * tile
    ca, sa, ci, si = tables
    ha, hi = ATT_HEAD_DIM // 2, IDX_DIM // 2
    col = lambda r: pl.BlockSpec((r, tile), lambda i: (0, i))
    rowb = lambda w: pl.BlockSpec((tile, w), lambda i: (i, 0))
    const = lambda r, w: pl.BlockSpec((r, w), lambda i: (0, 0))
    kvw = ATT_KV_HEADS * ATT_HEAD_DIM
    return pl.pallas_call(
        _dsa_prep_kernel,
        grid=(t // tile,),
        in_specs=[pl.BlockSpec((1, _ATT_ROWS, tile), lambda i: (i, 0, 0)), col(ha), col(ha), col(hi), col(hi),
                  const(ATT_HEAD_DIM, 1), const(ATT_HEAD_DIM, 1)],
        out_specs=[pl.BlockSpec((tile // Q_BLOCK, ATT_WIDTH, Q_BLOCK), lambda i: (i, 0, 0)), rowb(kvw),
                   pl.BlockSpec((tile // KEY_CHUNK, kvw, KEY_CHUNK), lambda i: (i, 0, 0)),
                   pl.BlockSpec((tile // Q_BLOCK, IDX_DIM, IDX_HEADS * Q_BLOCK), lambda i: (i, 0, 0)),
                   rowb(IDX_DIM), col(IDX_HEADS)],
        out_shape=[jax.ShapeDtypeStruct((t // Q_BLOCK, ATT_WIDTH, Q_BLOCK), BF16),
                   jax.ShapeDtypeStruct((t, kvw), BF16),
                   jax.ShapeDtypeStruct((t // KEY_CHUNK, kvw, KEY_CHUNK), BF16),
                   jax.ShapeDtypeStruct((t // Q_BLOCK, IDX_DIM, IDX_HEADS * Q_BLOCK), BF16),
                   jax.ShapeDtypeStruct((t, IDX_DIM), BF16),
                   jax.ShapeDtypeStruct((IDX_HEADS, t), F32)],
        compiler_params=_params(("parallel",)),
        name="dsa_prep",
    )(proj_t, ca, sa, ci, si, q_norm.reshape(ATT_HEAD_DIM, 1), k_norm.reshape(ATT_HEAD_DIM, 1))


def _bit_transpose32(words):
    a = list(words)
    for j, mask in ((16, 0x0000FFFF), (8, 0x00FF00FF), (4, 0x0F0F0F0F), (2, 0x33333333), (1, 0x55555555)):
        for k in range(32):
            if k & j == 0:
                t = (lax.shift_right_logical(a[k], np.int32(j)) ^ a[k + j]) & np.int32(mask)
                a[k + j] = a[k + j] ^ t
                a[k] = a[k] ^ lax.shift_left(t, np.int32(j))
    return a


def _dsa_kernel(ki_ref, k_ref, v_ref, qit_ref, qt_ref, wit_ref, o_ref, keys_ref, acc_ref, s_ref,
                planes_ref, *, topk, idx_bits):
    qb = Q_BLOCK
    kc = KEY_CHUNK
    j = pl.program_id(1)
    nk = (j * qb + qb + kc - 1) // kc
    row = lax.broadcasted_iota(I32, (kc, qb), 0)
    col = lax.broadcasted_iota(I32, (kc, qb), 1)
    q_pos = j * qb + col
    prows = kc // KEY_BITS

    def score_chunk(c, carry):
        ks = pl.multiple_of(c * kc, kc)
        s = _dot(ki_ref[pl.ds(ks, kc), :], qit_ref[0])
        acc = jnp.zeros((kc, qb), F32)
        for h in range(IDX_HEADS):
            acc = acc + wit_ref[h:h + 1, :] * jnp.maximum(s[:, h * qb:(h + 1) * qb], 0.0)
        acc = jnp.where(acc == 0.0, 0.0, acc)
        bits = pltpu.bitcast(acc, I32)
        key = jnp.where(bits < 0, bits ^ np.int32(0x7FFFFFFF), bits)
        key = jnp.where(ks + row <= q_pos, key, INT_MIN)
        keys_ref[pl.ds(ks, kc), :] = key
        u = key ^ INT_MIN
        for sub in range(prows // SUBLANES):
            r0 = sub * KEY_BITS * SUBLANES
            planes = _bit_transpose32([u[r0 + i * SUBLANES:r0 + (i + 1) * SUBLANES] for i in range(KEY_BITS)])
            ps = pl.multiple_of(c * prows + sub * SUBLANES, SUBLANES)
            for b in range(KEY_BITS):
                planes_ref[b, pl.ds(ps, SUBLANES), :] = planes[b]
        return carry

    lax.fori_loop(0, nk, score_chunk, 0)

    def clear_chunk(c, carry):
        ps = pl.multiple_of(c * prows, prows)
        for b in range(KEY_BITS):
            planes_ref[b, pl.ds(ps, prows), :] = jnp.zeros((prows, qb), I32)
        return carry

    lax.fori_loop(nk, planes_ref.shape[1] // prows, clear_chunk, 0)

    def count(hit):
        def body(c, acc):
            ks = pl.multiple_of(c * kc, kc)
            one = hit(keys_ref[pl.ds(ks, kc), :], ks + row)
            return acc + one.reshape(kc // SUBLANES, SUBLANES, qb).sum(axis=0)
        acc = lax.fori_loop(0, nk, body, jnp.zeros((SUBLANES, qb), I32))
        return acc.sum(axis=0, keepdims=True)

    nrows = planes_ref.shape[1]
    prow = lax.broadcasted_iota(I32, (nrows, qb), 0)
    active0 = jnp.where(prow < nk * prows, np.int32(-1), np.int32(0))

    def value_bit(i, carry):
        active, need, prefix = carry
        hit = active & planes_ref[KEY_BITS - 1 - i]
        n = jnp.sum(lax.population_count(hit), axis=0, keepdims=True)
        take = n >= need
        active = jnp.where(take, hit, active ^ hit)
        need = jnp.where(take, need, need - n)
        prefix = jnp.where(take, prefix | jnp.left_shift(np.int32(1), KEY_BITS - 1 - i), prefix)
        return active, need, prefix

    active, need, prefix = lax.fori_loop(
        0, KEY_BITS, value_bit, (active0, jnp.full((1, qb), topk, I32), jnp.zeros((1, qb), I32)))
    thr = prefix ^ INT_MIN

    n_tied = jnp.sum(lax.population_count(active), axis=0, keepdims=True)
    excess = jnp.max(jnp.where(thr > INT_MIN, n_tied - need, 0))

    def tie_search():
        def index_bit(i, prefix):
            cand = prefix | jnp.left_shift(np.int32(1), idx_bits - 1 - i)
            n = count(lambda key, idx: jnp.where(key == thr, jnp.where(idx < cand, 1, 0), 0))
            return jnp.where(n < need, cand, prefix)
        return lax.fori_loop(0, idx_bits, index_bit, jnp.zeros((1, qb), I32))

    last = lax.cond(excess > 0, tie_search, lambda: jnp.full((1, qb), 2 ** idx_bits, I32))

    rep = ATT_HEADS // ATT_KV_HEADS
    hd = ATT_HEAD_DIM
    wide = rep * qb
    groups = range(ATT_KV_HEADS)
    qgs = [jnp.concatenate([qt_ref[0, (g * rep + r) * hd:(g * rep + r + 1) * hd, :] for r in range(rep)], axis=1)
           for g in groups]

    def fold(x, op):
        return op(x.reshape(kc // SUBLANES, SUBLANES, wide), axis=0)

    def score_sweep(c, carry):
        ks = pl.multiple_of(c * kc, kc)
        key = keys_ref[pl.ds(ks, kc), :]
        idx = ks + row
        tied = jnp.where(key == thr, jnp.where(idx <= last, 0.0, NEG_BIG), NEG_BIG)
        b = jnp.where(idx <= q_pos, jnp.where(key > thr, 0.0, tied), NEG_BIG)
        b = jnp.concatenate([b] * rep, axis=1)
        out = []
        for g in groups:
            s = _dot(k_ref[pl.ds(ks, kc), g * hd:(g + 1) * hd], qgs[g]) + b
            s_ref[g, pl.ds(ks, kc), :] = s
            out.append(jnp.maximum(carry[g], fold(s, jnp.max)))
        return tuple(out)

    tops = lax.fori_loop(0, nk, score_sweep, tuple(jnp.full((SUBLANES, wide), NEG_BIG, F32) for _ in groups))
    tops = [jnp.max(m, axis=0, keepdims=True) for m in tops]
    acc_ref[...] = jnp.zeros_like(acc_ref)

    def value_sweep(c, carry):
        ks = pl.multiple_of(c * kc, kc)
        out = []
        for g in groups:
            p = jnp.exp(s_ref[g, pl.ds(ks, kc), :] - tops[g])
            acc_ref[g] += _dot(v_ref[c, g * hd:(g + 1) * hd, :], p.astype(BF16))
            out.append(carry[g] + fold(p, jnp.sum))
        return tuple(out)

    sums = lax.fori_loop(0, nk, value_sweep, tuple(jnp.zeros((SUBLANES, wide), F32) for _ in groups))
    for g in groups:
        o = acc_ref[g] / jnp.sum(sums[g], axis=0, keepdims=True)
        for r in range(rep):
            h = g * rep + r
            o_ref[:, h * hd:(h + 1) * hd] = o[:, r * qb:(r + 1) * qb].T.astype(o_ref.dtype)


def dsa_attention(ki, k, v3, qit, qt, wit, batch, seq):
    t = batch * seq
    nqb = seq // Q_BLOCK
    kvw = ATT_KV_HEADS * ATT_HEAD_DIM
    topk = min(TOPK_MAX, seq // 4)
    idx_bits = max(1, int(math.ceil(math.log2(seq))))
    assert seq % KEY_CHUNK == 0 and KEY_CHUNK % (KEY_BITS * SUBLANES) == 0 and KEY_CHUNK % Q_BLOCK == 0
    qcol = lambda r: pl.BlockSpec((r, Q_BLOCK), lambda b, j: (0, b * nqb + j))
    return pl.pallas_call(
        functools.partial(_dsa_kernel, topk=topk, idx_bits=idx_bits),
        grid=(batch, nqb),
        in_specs=[pl.BlockSpec((seq, IDX_DIM), lambda b, j: (b, 0)),
                  pl.BlockSpec((seq, kvw), lambda b, j: (b, 0)),
                  pl.BlockSpec((seq // KEY_CHUNK, kvw, KEY_CHUNK), lambda b, j: (b, 0, 0)),
                  pl.BlockSpec((1, IDX_DIM, IDX_HEADS * Q_BLOCK), lambda b, j: (b * nqb + j, 0, 0)),
                  pl.BlockSpec((1, ATT_WIDTH, Q_BLOCK), lambda b, j: (b * nqb + j, 0, 0)), qcol(IDX_HEADS)],
        out_specs=pl.BlockSpec((Q_BLOCK, ATT_WIDTH), lambda b, j: (b * nqb + j, 0)),
        out_shape=jax.ShapeDtypeStruct((t, ATT_WIDTH), BF16),
        scratch_shapes=[pltpu.VMEM((seq, Q_BLOCK), I32),
                        pltpu.VMEM((ATT_KV_HEADS, ATT_HEAD_DIM, (ATT_HEADS // ATT_KV_HEADS) * Q_BLOCK), F32),
                        pltpu.VMEM((ATT_KV_HEADS, seq, (ATT_HEADS // ATT_KV_HEADS) * Q_BLOCK), F32),
                        pltpu.VMEM((KEY_BITS, seq // KEY_BITS, Q_BLOCK), I32)],
        compiler_params=_params(("parallel", "arbitrary")),
        name="dsa_attention",
    )(ki, k, v3, qit, qt, wit)


def _merge_kernel(x_ref, ys_ref, ya_ref, gate_ref, wbs_ref, wba_ref, wo_ref, o_ref):
    d = x_ref.shape[1]
    gates = gate_ref[...]
    merged = (_sigmoid(gates[:, :d]) * _dot(ys_ref[...], wbs_ref[...])
              + _sigmoid(gates[:, d:]) * _dot(ya_ref[...], wba_ref[...]))
    o_ref[...] = x_ref[...] + _dot(merged.astype(BF16), wo_ref[...])


def merge_branches(x, y_ssm, y_att, gates, w_bs, w_ba, w_o, tm=512):
    t, d = x.shape
    tm = min(tm, t)
    tok = lambda w: pl.BlockSpec((tm, w), lambda i: (i, 0))
    const = lambda r, w: pl.BlockSpec((r, w), lambda i: (0, 0))
    return pl.pallas_call(
        _merge_kernel,
        grid=(t // tm,),
        in_specs=[tok(d), tok(D_INNER), tok(ATT_WIDTH), tok(2 * d),
                  const(D_INNER, d), const(ATT_WIDTH, d), const(d, d)],
        out_specs=tok(d),
        out_shape=jax.ShapeDtypeStruct((t, d), F32),
        compiler_params=_params(("parallel",)),
        name="merge_branches",
    )(x, y_ssm, y_att, gates, w_bs, w_ba, w_o)


def _top_rows(s, k, rows_out):
    rows = lax.broadcasted_iota(I32, (rows_out, s.shape[1]), 0)

    def body(r, carry):
        s, top = carry
        m = jnp.max(s, axis=0, keepdims=True)
        return jnp.where(s == m, -jnp.inf, s), jnp.where(rows == r, m, top)

    _, top = lax.fori_loop(0, k, body, (s, jnp.full((rows_out, s.shape[1]), -jnp.inf, F32)))
    return top


def _peer_route_kernel(x_ref, g_ref, wqt_ref, keys_ref, ht_ref, tau_ref, e1_ref, e2_ref):
    tm = x_ref.shape[0]
    h = _rms(x_ref[...], g_ref[...])
    ht_ref[0] = h.T.astype(BF16)
    qt = lax.dot_general(wqt_ref[...], h.astype(BF16), _NT, preferred_element_type=F32).astype(BF16)
    half = PEER_QDIM // 2
    k = PEER_TOPK
    k1 = k + 1
    rows_out = -(-k1 // SUBLANES) * SUBLANES
    sub8 = lax.broadcasted_iota(I32, (SUBLANES, tm), 0)
    for hh in range(PEER_HEADS):
        s1 = _dot(keys_ref[2 * hh], qt[(2 * hh) * half:(2 * hh + 1) * half, :])
        s2 = _dot(keys_ref[2 * hh + 1], qt[(2 * hh + 1) * half:(2 * hh + 2) * half, :])
        a = _top_rows(s1, k1, rows_out)
        b = _top_rows(s2, k1, rows_out)
        assert k1 // (SUBLANES + 1) <= 1
        tiles = [a[r:r + SUBLANES] + b[0:1] for r in range(SUBLANES, rows_out, SUBLANES)]
        for i in range(SUBLANES):
            lim = k1 // (i + 1)
            for r in range(0, lim, SUBLANES):
                tile = b[r:r + SUBLANES] + a[i:i + 1]
                tiles.append(tile if lim - r >= SUBLANES else jnp.where(sub8 < lim - r, tile, -jnp.inf))
        cand = jnp.concatenate(tiles, axis=0)
        best = a[0:1] + b[0:1]

        def body(r, carry):
            cand, zsum, kth, nxt = carry
            m = jnp.max(cand, axis=0, keepdims=True)
            zsum = zsum + jnp.where(r < k, jnp.exp(m - best), 0.0)
            return jnp.where(cand == m, -jnp.inf, cand), zsum, jnp.where(r == k - 1, m, kth), m

        _, zsum, kth, nxt = lax.fori_loop(0, k1, body, (cand, jnp.zeros((1, tm), F32), best, best))
        cut = 0.5 * (kth + nxt)
        r0 = hh * N_KEYS
        tau_ref[0, r0:r0 + N_KEYS, :] = jnp.exp((cut - b[0:1]) - s1)
        e1_ref[0, r0:r0 + N_KEYS, :] = jnp.exp(s1 - a[0:1]) * (0.5 / zsum)
        e2_ref[0, r0:r0 + N_KEYS, :] = jnp.exp(s2 - b[0:1])


def peer_route(x, g, wq_t, keys, tm=256):
    t, d = x.shape
    tm = min(tm, t)
    rows = PEER_HEADS * N_KEYS
    blk = lambda r: pl.BlockSpec((1, r, tm), lambda i: (i, 0, 0))
    return pl.pallas_call(
        _peer_route_kernel,
        grid=(t // tm,),
        in_specs=[pl.BlockSpec((tm, d), lambda i: (i, 0)),
                  pl.BlockSpec((1, d), lambda i: (0, 0)),
                  pl.BlockSpec(wq_t.shape, lambda i: (0, 0)),
                  pl.BlockSpec(keys.shape, lambda i: (0, 0, 0))],
        out_specs=[blk(d), blk(rows), blk(rows), blk(rows)],
        out_shape=[jax.ShapeDtypeStruct((t // tm, d, tm), BF16)] + [jax.ShapeDtypeStruct((t // tm, rows, tm), F32)] * 3,
        compiler_params=_params(("parallel",)),
        name="peer_route",
    )(x, g, wq_t, keys)


PEER_ROWS = 32
PEER_BLOCK_KEYS = 4


def _peer_dense_kernel(x_ref, ht_ref, tau_ref, e1_ref, e2_ref, u_ref, vt_ref, o_ref, acc_ref, hid_ref, act_ref,
                       *, ib, nblk):
    j = pl.program_id(1)
    nsub, _, tsub = ht_ref.shape
    tt = nsub * tsub
    cur = j % 2
    prev = 1 - cur

    @pl.when(j == 0)
    def _():
        acc_ref[...] = jnp.zeros_like(acc_ref)

    def second_matmul():
        acc_ref[...] += _dot(vt_ref[0], act_ref[cur])

    def first_matmul():
        ht = jnp.concatenate([ht_ref[sb] for sb in range(nsub)], axis=1)
        hid_ref[cur] = _dot(u_ref[...], ht)

    def gate_stage():
        blk = j - 1
        for ii in range(ib):
            i1 = blk * ib + ii
            taus = [[tau_ref[sb, pl.ds(hh * N_KEYS + i1, 1), :] for hh in range(PEER_HEADS)] for sb in range(nsub)]
            e1s = [[e1_ref[sb, pl.ds(hh * N_KEYS + i1, 1), :] for hh in range(PEER_HEADS)] for sb in range(nsub)]
            for lc in range(tt // LANES):
                cols = slice(lc * LANES, (lc + 1) * LANES)
                sb = lc * LANES // tsub
                sub = slice(lc * LANES - sb * tsub, (lc + 1) * LANES - sb * tsub)
                tau_b = [jnp.broadcast_to(taus[sb][hh][:, sub], (PEER_ROWS, LANES)) for hh in range(PEER_HEADS)]
                e1_b = [jnp.broadcast_to(e1s[sb][hh][:, sub], (PEER_ROWS, LANES)) for hh in range(PEER_HEADS)]
                for rb in range(N_KEYS // PEER_ROWS):
                    gate = jnp.zeros((PEER_ROWS, LANES), F32)
                    for hh in range(PEER_HEADS):
                        r0 = hh * N_KEYS + rb * PEER_ROWS
                        e2 = e2_ref[sb, r0:r0 + PEER_ROWS, sub]
                        gate = gate + jnp.where(e2 >= tau_b[hh], e2, 0.0) * e1_b[hh]
                    rows = slice(ii * N_KEYS + rb * PEER_ROWS, ii * N_KEYS + (rb + 1) * PEER_ROWS)
                    hb = hid_ref[prev, rows, cols]
                    act = hb * (1.0 + lax.erf(hb * np.float32(math.sqrt(0.5))))
                    act_ref[prev, rows, cols] = (act * gate).astype(BF16)

    interior = jnp.logical_and(j >= 2, j < nblk)

    @pl.when(interior)
    def _():
        second_matmul()
        gate_stage()
        first_matmul()

    @pl.when(jnp.logical_and(jnp.logical_not(interior), j >= 2))
    def _():
        second_matmul()

    @pl.when(jnp.logical_and(jnp.logical_not(interior), jnp.logical_and(j >= 1, j <= nblk)))
    def _():
        gate_stage()

    @pl.when(jnp.logical_and(jnp.logical_not(interior), j < nblk))
    def _():
        first_matmul()

    @pl.when(j == pl.num_programs(1) - 1)
    def _():
        o_ref[...] = x_ref[...] + acc_ref[...].T


def peer_dense(x, h_t, tau, e1, e2, u, v_blk, tt=512):
    t, d = x.shape
    tt = min(tt, t)
    rows = PEER_HEADS * N_KEYS
    nblk = v_blk.shape[0]
    ib = N_KEYS // nblk
    tsub = h_t.shape[2]
    col = lambda r: pl.BlockSpec((tt // tsub, r, tsub), lambda i, j: (i, 0, 0))
    tok = pl.BlockSpec((tt, d), lambda i, j: (i, 0))
    return pl.pallas_call(
        functools.partial(_peer_dense_kernel, ib=ib, nblk=nblk),
        grid=(t // tt, nblk + 2),
        in_specs=[tok, col(d), col(rows), col(rows), col(rows),
                  pl.BlockSpec((ib * N_KEYS, d), lambda i, j: (jnp.minimum(j, nblk - 1), 0)),
                  pl.BlockSpec((1, d, ib * N_KEYS), lambda i, j: (jnp.clip(j - 2, 0, nblk - 1), 0, 0))],
        out_specs=tok,
        out_shape=jax.ShapeDtypeStruct((t, d), F32),
        scratch_shapes=[pltpu.VMEM((d, tt), F32), pltpu.VMEM((2, ib * N_KEYS, tt), F32),
                        pltpu.VMEM((2, ib * N_KEYS, tt), BF16)],
        compiler_params=_params(("parallel", "arbitrary")),
        name="peer_dense",
    )(x, h_t, tau, e1, e2, u, v_blk)


def _ple_kernel(x_ref, p_ref, g_ref, wg_ref, wp_ref, o_ref):
    x = x_ref[...]
    h = _rms(x, g_ref[...]).astype(BF16)
    gate = _sigmoid(_dot(h, wg_ref[...]))
    o_ref[...] = x + gate * _dot(p_ref[...].astype(BF16), wp_ref[...])


def ple_block(x, p, g, w_gate, w_proj, tm=512):
    t, d = x.shape
    tm = min(tm, t)
    pd = p.shape[1]
    return pl.pallas_call(
        _ple_kernel,
        grid=(t // tm,),
        in_specs=[pl.BlockSpec((tm, d), lambda i: (i, 0)),
                  pl.BlockSpec((tm, pd), lambda i: (i, 0)),
                  pl.BlockSpec((1, d), lambda i: (0, 0)),
                  pl.BlockSpec((d, d), lambda i: (0, 0)),
                  pl.BlockSpec((pd, d), lambda i: (0, 0))],
        out_specs=pl.BlockSpec((tm, d), lambda i: (i, 0)),
        out_shape=jax.ShapeDtypeStruct((t, d), F32),
        compiler_params=_params(("parallel",)),
        name="ple_block",
    )(x, p, g, w_gate, w_proj)


def _split_w_in(w):
    sizes = [D_INNER, CONV_DIM, SSM_HEADS, ATT_WIDTH, ATT_KV_HEADS * ATT_HEAD_DIM,
             ATT_KV_HEADS * ATT_HEAD_DIM, IDX_HEADS * IDX_DIM, IDX_DIM, IDX_HEADS, 2 * w.shape[0]]
    out, acc = [], 0
    for s in sizes:
        out.append(w[:, acc:acc + s])
        acc += s
    return out


def kernel(x, p, positions, norm_mix, w_in, conv_w, conv_b, dt_bias, a_log, d_skip, ssm_norm, q_norm, k_norm,
           w_branch, w_out, norm_ffn, peer_wq, peer_keys, peer_u, peer_v, norm_ple, w_ple_gate, w_ple_proj):
    batch, seq, d = x.shape
    t = batch * seq
    depth = w_in.shape[0]
    tables = rope_tables_t(positions)
    xf = x.reshape(t, d)
    for i in range(depth):
        wz, wxbc, wdt, wq, wk, wv, wqi, wki, wwi, wg = _split_w_in(w_in[i])
        w_dt = jnp.pad(wdt, ((0, 0), (0, LANES - SSM_HEADS))).astype(BF16)
        w_att_t = jnp.transpose(jnp.concatenate([wq, wk, wv, wqi, wki, wwi], axis=1)).astype(BF16)
        g_mix = norm_mix[i].reshape(1, d)

        zx = norm_matmul(xf, g_mix, jnp.concatenate([wz, wxbc], axis=1).astype(BF16))
        gates_dt = norm_matmul(xf, g_mix, jnp.concatenate([wg.astype(BF16), w_dt], axis=1))
        proj_t = norm_matmul_t(xf, g_mix, w_att_t)

        y_ssm = ssm_branch(zx, gates_dt, conv_w[i], conv_b[i], dt_bias[i], a_log[i], d_skip[i],
                           ssm_norm[i], batch, seq)
        qt, k, v3, qit, ki, wit = dsa_prep(proj_t, tables, q_norm[i], k_norm[i])
        y_att = dsa_attention(ki, k, v3, qit, qt, wit, batch, seq)
        xf = merge_branches(xf, y_ssm, y_att, gates_dt, w_branch[i, :D_INNER].astype(BF16),
                            w_branch[i, D_INNER:].astype(BF16), w_out[i].astype(BF16))

        keys = peer_keys[i].reshape(PEER_HEADS * 2, N_KEYS, PEER_QDIM // 2).astype(BF16)
        h_t, tau, e1, e2 = peer_route(xf, norm_ffn[i].reshape(1, d), jnp.transpose(peer_wq[i]).astype(BF16), keys)
        v_blk = jnp.transpose(peer_v[i].reshape(N_KEYS // PEER_BLOCK_KEYS, PEER_BLOCK_KEYS * N_KEYS, d), (0, 2, 1))
        xf = peer_dense(xf, h_t, tau, e1, e2, peer_u[i].astype(BF16), v_blk.astype(BF16))

        xf = ple_block(xf, p[i].reshape(t, -1), norm_ple[i].reshape(1, d),
                       w_ple_gate[i].astype(BF16), w_ple_proj[i].astype(BF16))
    return xf.reshape(batch, seq, d)
```

```python
import functools
import math

import numpy as np
import jax
import jax.numpy as jnp
from jax import lax
from jax.experimental import pallas as pl
from jax.experimental.pallas import tpu as pltpu

F32 = jnp.float32
BF16 = jnp.bfloat16
I32 = jnp.int32

EPS = 1e-6
ROPE_THETA = 10000.0

SSM_HEADS = 32
SSM_HEAD_DIM = 64
D_INNER = SSM_HEADS * SSM_HEAD_DIM
SSM_GROUPS = 4
D_STATE = 128
CONV_WIDTH = 4
CHUNK = 128
CONV_DIM = D_INNER + 2 * SSM_GROUPS * D_STATE
ATT_HEADS = 8
ATT_KV_HEADS = 2
ATT_HEAD_DIM = 128
ATT_WIDTH = ATT_HEADS * ATT_HEAD_DIM
IDX_HEADS = 8
IDX_DIM = 64
TOPK_MAX = 256
Q_BLOCK = 128
KEY_CHUNK = 512
KEY_BITS = 32
PEER_HEADS = 8
N_KEYS = 128
PEER_TOPK = 16
PEER_QDIM = 256

LANES = 128
SUBLANES = 8
VMEM_LIMIT = 56 * 1024 * 1024
INT_MIN = np.int32(-2 ** 31)
NEG_BIG = -1e30

_NT = (((1,), (1,)), ((), ()))


def _params(sem):
    return pltpu.CompilerParams(dimension_semantics=sem, vmem_limit_bytes=VMEM_LIMIT)


def _rms(x, g):
    return x * lax.rsqrt(jnp.mean(x * x, axis=-1, keepdims=True) + EPS) * g


def _sigmoid(x):
    return 1.0 / (1.0 + jnp.exp(-x))


def _split3(a):
    a1 = a.astype(BF16)
    r1 = a - a1.astype(F32)
    a2 = r1.astype(BF16)
    a3 = (r1 - a2.astype(F32)).astype(BF16)
    return a1, a2, a3


def _dot(a, b):
    return jnp.dot(a, b, preferred_element_type=F32)


def _norm_mm_kernel(x_ref, g_ref, w_ref, o_ref):
    h = _rms(x_ref[...], g_ref[...]).astype(BF16)
    o_ref[...] = _dot(h, w_ref[...]).astype(o_ref.dtype)


def norm_matmul(x, g, w, out_dtype=F32, tm=512):
    m, k = x.shape
    n = w.shape[1]
    return pl.pallas_call(
        _norm_mm_kernel,
        grid=(m // tm,),
        in_specs=[pl.BlockSpec((tm, k), lambda i: (i, 0)),
                  pl.BlockSpec((1, k), lambda i: (0, 0)),
                  pl.BlockSpec((k, n), lambda i: (0, 0))],
        out_specs=pl.BlockSpec((tm, n), lambda i: (i, 0)),
        out_shape=jax.ShapeDtypeStruct((m, n), out_dtype),
        compiler_params=_params(("parallel",)),
        name="norm_matmul",
    )(x, g, w)


def _norm_mm_t_kernel(x_ref, g_ref, wt_ref, o_ref):
    h = _rms(x_ref[...], g_ref[...]).astype(BF16)
    o_ref[...] = lax.dot_general(wt_ref[...], h, _NT, preferred_element_type=F32)


def norm_matmul_t(x, g, wt, tm=512):
    m, k = x.shape
    n = wt.shape[0]
    return pl.pallas_call(
        _norm_mm_t_kernel,
        grid=(m // tm,),
        in_specs=[pl.BlockSpec((tm, k), lambda i: (i, 0)),
                  pl.BlockSpec((1, k), lambda i: (0, 0)),
                  pl.BlockSpec((n, k), lambda i: (0, 0))],
        out_specs=pl.BlockSpec((n, tm), lambda i: (0, i)),
        out_shape=jax.ShapeDtypeStruct((n, m), F32),
        compiler_params=_params(("parallel",)),
        name="norm_matmul_t",
    )(x, g, wt)


def _ssm_kernel(zx_ref, dt_ref, convw_ref, convb_ref, dtb_ref, alog_ref,
                dskip_ref, normg_ref, expand_ref, y_ref, carry_ref, state_ref):
    @pl.when(pl.program_id(1) == 0)
    def _():
        carry_ref[...] = jnp.zeros_like(carry_ref)
        state_ref[...] = jnp.zeros_like(state_ref)

    xbc = zx_ref[:, D_INNER:]
    tail = carry_ref[...]
    row8 = lax.broadcasted_iota(I32, (SUBLANES, CONV_DIM), 0)
    conv = convb_ref[...] + convw_ref[CONV_WIDTH - 1:CONV_WIDTH, :] * xbc
    for j in range(CONV_WIDTH - 1):
        k = CONV_WIDTH - 1 - j
        shifted = pltpu.roll(xbc, k, 0)
        head = jnp.where(row8 < k, pltpu.roll(tail, k, 0), shifted[:SUBLANES])
        conv = conv + convw_ref[j:j + 1, :] * jnp.concatenate([head, shifted[SUBLANES:]], axis=0)
    carry_ref[...] = xbc[CHUNK - SUBLANES:]
    xc = conv * _sigmoid(conv)
    xs = xc[:, :D_INNER]
    gw = SSM_GROUPS * D_STATE
    bm = xc[:, D_INNER:D_INNER + gw]
    cm = xc[:, D_INNER + gw:]

    lane = lax.broadcasted_iota(I32, (CHUNK, LANES), 1)
    row = lax.broadcasted_iota(I32, (CHUNK, LANES), 0)
    head_lane = lane < SSM_HEADS
    dt = jax.nn.softplus(dt_ref[...] + dtb_ref[...])
    adt = jnp.where(head_lane, dt * (-jnp.exp(alog_ref[...])), 0.0)
    tril = jnp.where(row >= lane, 1.0, 0.0).astype(BF16)
    cs = sum(_dot(tril, part) for part in _split3(adt))
    cs_row = cs.T
    cs_last = cs[CHUNK - 1:CHUNK, :]
    ecs = jnp.exp(cs)
    decay = jnp.exp(cs_last - cs)

    expand = expand_ref[...]

    def widen(a):
        return sum(_dot(part, expand) for part in _split3(a))

    dt_w = widen(dt)
    ecs_w = widen(ecs)
    decay_w = widen(decay)
    xdt = xs * dt_w
    xdt_b = xdt.astype(BF16)
    xdec_b = (xdt * decay_w).astype(BF16)
    causal = row >= lane
    pair_lo = lane < SSM_HEAD_DIM

    gh = SSM_HEADS // SSM_GROUPS
    gcols = gh * SSM_HEAD_DIM
    ys = []
    for g in range(SSM_GROUPS):
        bg = bm[:, g * D_STATE:(g + 1) * D_STATE]
        cg = cm[:, g * D_STATE:(g + 1) * D_STATE].astype(BF16)
        cb = lax.dot_general(cg, bg.astype(BF16), _NT, preferred_element_type=F32)
        st = state_ref[g]
        y_off = _dot(cg, st.astype(BF16))
        new_st = _dot(bg.T.astype(BF16), xdec_b[:, g * gcols:(g + 1) * gcols])
        state_ref[g] = st * ecs_w[CHUNK - 1:CHUNK, g * gcols:(g + 1) * gcols] + new_st
        pieces = []
        for pr in range(gh // 2):
            outs = []
            for sub in range(2):
                h = g * gh + pr * 2 + sub
                lm = jnp.where(causal, jnp.exp(cs[:, h:h + 1] - cs_row[h:h + 1, :]), 0.0)
                mm = (cb * lm).astype(BF16)
                c0 = (h // 2) * LANES
                outs.append(_dot(mm, xdt_b[:, c0:c0 + LANES]))
            pieces.append(jnp.where(pair_lo, outs[0], outs[1]))
        y_diag = jnp.concatenate(pieces, axis=1)
        ys.append(y_diag + y_off * ecs_w[:, g * gcols:(g + 1) * gcols])
    y = jnp.concatenate(ys, axis=1) + dskip_ref[...] * xs
    z = zx_ref[:, :D_INNER]
    y = y * (z * _sigmoid(z))
    outs = []
    for g in range(SSM_GROUPS):
        yg = y[:, g * gcols:(g + 1) * gcols]
        outs.append(yg * lax.rsqrt(jnp.mean(yg * yg, axis=-1, keepdims=True) + EPS))
    y_ref[...] = (jnp.concatenate(outs, axis=1) * normg_ref[...]).astype(y_ref.dtype)


def ssm_branch(zx, gates_dt, conv_w, conv_b, dt_bias, a_log, d_skip, norm_g, batch, seq):
    t = batch * seq
    nchunk = seq // CHUNK
    pad = LANES - SSM_HEADS
    convw_t = jnp.transpose(conv_w)
    dtb = jnp.pad(dt_bias, (0, pad)).reshape(1, LANES)
    alog = jnp.pad(a_log, (0, pad)).reshape(1, LANES)
    dskip = jnp.repeat(d_skip, SSM_HEAD_DIM).reshape(1, D_INNER)
    expand = (np.arange(LANES)[:, None] == (np.arange(D_INNER)[None, :] // SSM_HEAD_DIM))
    expand = jnp.asarray(expand, dtype=BF16)
    tok = lambda w: pl.BlockSpec((CHUNK, w), lambda b, c: (b * nchunk + c, 0))
    const = lambda r, w: pl.BlockSpec((r, w), lambda b, c: (0, 0))
    dt_block = (gates_dt.shape[1] - LANES) // LANES
    return pl.pallas_call(
        _ssm_kernel,
        grid=(batch, nchunk),
        in_specs=[tok(D_INNER + CONV_DIM), pl.BlockSpec((CHUNK, LANES), lambda b, c: (b * nchunk + c, dt_block)),
                  const(CONV_WIDTH, CONV_DIM), const(1, CONV_DIM), const(1, LANES),
                  const(1, LANES), const(1, D_INNER), const(1, D_INNER),
                  const(LANES, D_INNER)],
        out_specs=tok(D_INNER),
        out_shape=jax.ShapeDtypeStruct((t, D_INNER), BF16),
        scratch_shapes=[pltpu.VMEM((SUBLANES, CONV_DIM), F32),
                        pltpu.VMEM((SSM_GROUPS, D_STATE, D_INNER // SSM_GROUPS), F32)],
        compiler_params=_params(("parallel", "arbitrary")),
        name="ssm_branch",
    )(zx, gates_dt, convw_t, conv_b.reshape(1, CONV_DIM), dtb, alog, dskip,
      norm_g.reshape(1, D_INNER), expand)


def _rope_kernel(pos_ref, inva_ref, invi_ref, ca_ref, sa_ref, ci_ref, si_ref):
    pos = pos_ref[...]
    ang_a = inva_ref[...] * pos
    ang_i = invi_ref[...] * pos
    ca_ref[...] = jnp.cos(ang_a)
    sa_ref[...] = jnp.sin(ang_a)
    ci_ref[...] = jnp.cos(ang_i)
    si_ref[...] = jnp.sin(ang_i)


def rope_tables_t(positions, tile=2048):
    t = positions.size
    tile = min(tile, t)
    pos = positions.reshape(1, t).astype(F32)
    ha, hi = ATT_HEAD_DIM // 2, IDX_DIM // 2
    inv_a = (1.0 / (ROPE_THETA ** (jnp.arange(0, ATT_HEAD_DIM, 2, dtype=F32) / ATT_HEAD_DIM))).reshape(ha, 1)
    inv_i = (1.0 / (ROPE_THETA ** (jnp.arange(0, IDX_DIM, 2, dtype=F32) / IDX_DIM))).reshape(hi, 1)
    col = lambda r: pl.BlockSpec((r, tile), lambda i: (0, i))
    return pl.pallas_call(
        _rope_kernel,
        grid=(t // tile,),
        in_specs=[col(1), pl.BlockSpec((ha, 1), lambda i: (0, 0)), pl.BlockSpec((hi, 1), lambda i: (0, 0))],
        out_specs=[col(ha), col(ha), col(hi), col(hi)],
        out_shape=[jax.ShapeDtypeStruct((ha, t), F32)] * 2 + [jax.ShapeDtypeStruct((hi, t), F32)] * 2,
        compiler_params=_params(("parallel",)),
        name="rope_tables",
    )(pos, inv_a, inv_i)


_Q0 = 0
_K0 = _Q0 + ATT_WIDTH
_V0 = _K0 + ATT_KV_HEADS * ATT_HEAD_DIM
_QI0 = _V0 + ATT_KV_HEADS * ATT_HEAD_DIM
_KI0 = _QI0 + IDX_HEADS * IDX_DIM
_WI0 = _KI0 + IDX_DIM
_ATT_ROWS = _WI0 + IDX_HEADS


def _rope_rows(x, cos, sin):
    half = x.shape[0] // 2
    x1, x2 = x[:half], x[half:]
    return jnp.concatenate([x1 * cos - x2 * sin, x1 * sin + x2 * cos], axis=0)


def _head_norm_rows(x, g):
    return x * lax.rsqrt(jnp.mean(x * x, axis=0, keepdims=True) + EPS) * g


def _dsa_prep_kernel(p_ref, ca_ref, sa_ref, ci_ref, si_ref, qg_ref, kg_ref,
                     qt_ref, k_ref, v_ref, qit_ref, ki_ref, wit_ref):
    tile = p_ref.shape[1]
    ca, sa, ci, si = ca_ref[...], sa_ref[...], ci_ref[...], si_ref[...]
    hd = ATT_HEAD_DIM
    for h in range(ATT_HEADS):
        x = _head_norm_rows(p_ref[_Q0 + h * hd:_Q0 + (h + 1) * hd, :], qg_ref[...])
        qt_ref[h * hd:(h + 1) * hd, :] = (_rope_rows(x, ca, sa) * (hd ** -0.5)).astype(BF16)
    for h in range(ATT_KV_HEADS):
        x = _head_norm_rows(p_ref[_K0 + h * hd:_K0 + (h + 1) * hd, :], kg_ref[...])
        k_ref[:, h * hd:(h + 1) * hd] = _rope_rows(x, ca, sa).T.astype(BF16)
    for s in range(tile // KEY_CHUNK):
        v_ref[s] = p_ref[_V0:_V0 + ATT_KV_HEADS * hd, s * KEY_CHUNK:(s + 1) * KEY_CHUNK].astype(BF16)
    for h in range(IDX_HEADS):
        x = _rope_rows(p_ref[_QI0 + h * IDX_DIM:_QI0 + (h + 1) * IDX_DIM, :], ci, si) * (IDX_DIM ** -0.5)
        x = x.astype(BF16)
        for s in range(tile // Q_BLOCK):
            qit_ref[s, :, h * Q_BLOCK:(h + 1) * Q_BLOCK] = x[:, s * Q_BLOCK:(s + 1) * Q_BLOCK]
    x = _rope_rows(p_ref[_KI0:_KI0 + IDX_DIM, :], ci, si)
    zpad = jnp.zeros((LANES - IDX_DIM, tile), F32)
    ki_ref[...] = jnp.concatenate([x, zpad], axis=0).T[:, :IDX_DIM].astype(BF16)
    wit_ref[...] = p_ref[_WI0:_WI0 + IDX_HEADS, :] * (IDX_HEADS ** -0.5)


def dsa_prep(proj_t, tables, q_norm, k_norm, tile=512):
    t = proj_t.shape[1]
    tile = min(tile, t)
    ca, sa, ci, si = tables
    ha, hi = ATT_HEAD_DIM // 2, IDX_DIM // 2
    col = lambda r: pl.BlockSpec((r, tile), lambda i: (0, i))
    rowb = lambda w: pl.BlockSpec((tile, w), lambda i: (i, 0))
    const = lambda r, w: pl.BlockSpec((r, w), lambda i: (0, 0))
    kvw = ATT_KV_HEADS * ATT_HEAD_DIM
    return pl.pallas_call(
        _dsa_prep_kernel,
        grid=(t // tile,),
        in_specs=[col(_ATT_ROWS), col(ha), col(ha), col(hi), col(hi),
                  const(ATT_HEAD_DIM, 1), const(ATT_HEAD_DIM, 1)],
        out_specs=[col(ATT_WIDTH), rowb(kvw),
                   pl.BlockSpec((tile // KEY_CHUNK, kvw, KEY_CHUNK), lambda i: (i, 0, 0)),
                   pl.BlockSpec((tile // Q_BLOCK, IDX_DIM, IDX_HEADS * Q_BLOCK), lambda i: (i, 0, 0)),
                   rowb(IDX_DIM), col(IDX_HEADS)],
        out_shape=[jax.ShapeDtypeStruct((ATT_WIDTH, t), BF16),
                   jax.ShapeDtypeStruct((t, kvw), BF16),
                   jax.ShapeDtypeStruct((t // KEY_CHUNK, kvw, KEY_CHUNK), BF16),
                   jax.ShapeDtypeStruct((t // Q_BLOCK, IDX_DIM, IDX_HEADS * Q_BLOCK), BF16),
                   jax.ShapeDtypeStruct((t, IDX_DIM), BF16),
                   jax.ShapeDtypeStruct((IDX_HEADS, t), F32)],
        compiler_params=_params(("parallel",)),
        name="dsa_prep",
    )(proj_t, ca, sa, ci, si, q_norm.reshape(ATT_HEAD_DIM, 1), k_norm.reshape(ATT_HEAD_DIM, 1))


def _bit_transpose32(words):
    a = list(words)
    for j, mask in ((16, 0x0000FFFF), (8, 0x00FF00FF), (4, 0x0F0F0F0F), (2, 0x33333333), (1, 0x55555555)):
        for k in range(32):
            if k & j == 0:
                t = (lax.shift_right_logical(a[k], np.int32(j)) ^ a[k + j]) & np.int32(mask)
                a[k + j] = a[k + j] ^ t
                a[k] = a[k] ^ lax.shift_left(t, np.int32(j))
    return a


def _dsa_kernel(ki_ref, k_ref, v_ref, qit_ref, qt_ref, wit_ref, o_ref, keys_ref, acc_ref, s_ref,
                planes_ref, *, topk, idx_bits):
    qb = Q_BLOCK
    kc = KEY_CHUNK
    j = pl.program_id(1)
    nk = (j * qb + qb + kc - 1) // kc
    row = lax.broadcasted_iota(I32, (kc, qb), 0)
    col = lax.broadcasted_iota(I32, (kc, qb), 1)
    q_pos = j * qb + col
    prows = kc // KEY_BITS

    def score_chunk(c, carry):
        ks = pl.multiple_of(c * kc, kc)
        s = _dot(ki_ref[pl.ds(ks, kc), :], qit_ref[0])
        acc = jnp.zeros((kc, qb), F32)
        for h in range(IDX_HEADS):
            acc = acc + wit_ref[h:h + 1, :] * jnp.maximum(s[:, h * qb:(h + 1) * qb], 0.0)
        acc = jnp.where(acc == 0.0, 0.0, acc)
        bits = pltpu.bitcast(acc, I32)
        key = jnp.where(bits < 0, bits ^ np.int32(0x7FFFFFFF), bits)
        key = jnp.where(ks + row <= q_pos, key, INT_MIN)
        keys_ref[pl.ds(ks, kc), :] = key
        u = key ^ INT_MIN
        for sub in range(prows // SUBLANES):
            r0 = sub * KEY_BITS * SUBLANES
            planes = _bit_transpose32([u[r0 + i * SUBLANES:r0 + (i + 1) * SUBLANES] for i in range(KEY_BITS)])
            ps = pl.multiple_of(c * prows + sub * SUBLANES, SUBLANES)
            for b in range(KEY_BITS):
                planes_ref[b, pl.ds(ps, SUBLANES), :] = planes[b]
        return carry

    lax.fori_loop(0, nk, score_chunk, 0)

    def clear_chunk(c, carry):
        ps = pl.multiple_of(c * prows, prows)
        for b in range(KEY_BITS):
            planes_ref[b, pl.ds(ps, prows), :] = jnp.zeros((prows, qb), I32)
        return carry

    lax.fori_loop(nk, planes_ref.shape[1] // prows, clear_chunk, 0)

    def count(hit):
        def body(c, acc):
            ks = pl.multiple_of(c * kc, kc)
            one = hit(keys_ref[pl.ds(ks, kc), :], ks + row)
            return acc + one.reshape(kc // SUBLANES, SUBLANES, qb).sum(axis=0)
        acc = lax.fori_loop(0, nk, body, jnp.zeros((SUBLANES, qb), I32))
        return acc.sum(axis=0, keepdims=True)

    nrows = planes_ref.shape[1]
    prow = lax.broadcasted_iota(I32, (nrows, qb), 0)
    active0 = jnp.where(prow < nk * prows, np.int32(-1), np.int32(0))

    def value_bit(i, carry):
        active, need, prefix = carry
        hit = active & planes_ref[KEY_BITS - 1 - i]
        n = jnp.sum(lax.population_count(hit), axis=0, keepdims=True)
        take = n >= need
        active = jnp.where(take, hit, active ^ hit)
        need = jnp.where(take, need, need - n)
        prefix = jnp.where(take, prefix | jnp.left_shift(np.int32(1), KEY_BITS - 1 - i), prefix)
        return active, need, prefix

    active, need, prefix = lax.fori_loop(
        0, KEY_BITS, value_bit, (active0, jnp.full((1, qb), topk, I32), jnp.zeros((1, qb), I32)))
    thr = prefix ^ INT_MIN

    n_tied = jnp.sum(lax.population_count(active), axis=0, keepdims=True)
    excess = jnp.max(jnp.where(thr > INT_MIN, n_tied - need, 0))

    def tie_search():
        def index_bit(i, prefix):
            cand = prefix | jnp.left_shift(np.int32(1), idx_bits - 1 - i)
            n = count(lambda key, idx: jnp.where(key == thr, jnp.where(idx < cand, 1, 0), 0))
            return jnp.where(n < need, cand, prefix)
        return lax.fori_loop(0, idx_bits, index_bit, jnp.zeros((1, qb), I32))

    last = lax.cond(excess > 0, tie_search, lambda: jnp.full((1, qb), 2 ** idx_bits, I32))

    rep = ATT_HEADS // ATT_KV_HEADS
    hd = ATT_HEAD_DIM
    wide = rep * qb
    groups = range(ATT_KV_HEADS)
    qgs = [jnp.concatenate([qt_ref[(g * rep + r) * hd:(g * rep + r + 1) * hd, :] for r in range(rep)], axis=1)
           for g in groups]

    def fold(x, op):
        return op(x.reshape(kc // SUBLANES, SUBLANES, wide), axis=0)

    def score_sweep(c, carry):
        ks = pl.multiple_of(c * kc, kc)
        key = keys_ref[pl.ds(ks, kc), :]
        idx = ks + row
        tied = jnp.where(key == thr, jnp.where(idx <= last, 0.0, NEG_BIG), NEG_BIG)
        b = jnp.where(idx <= q_pos, jnp.where(key > thr, 0.0, tied), NEG_BIG)
        b = jnp.concatenate([b] * rep, axis=1)
        out = []
        for g in groups:
            s = _dot(k_ref[pl.ds(ks, kc), g * hd:(g + 1) * hd], qgs[g]) + b
            s_ref[g, pl.ds(ks, kc), :] = s
            out.append(jnp.maximum(carry[g], fold(s, jnp.max)))
        return tuple(out)

    tops = lax.fori_loop(0, nk, score_sweep, tuple(jnp.full((SUBLANES, wide), NEG_BIG, F32) for _ in groups))
    tops = [jnp.max(m, axis=0, keepdims=True) for m in tops]
    acc_ref[...] = jnp.zeros_like(acc_ref)

    def value_sweep(c, carry):
        ks = pl.multiple_of(c * kc, kc)
        out = []
        for g in groups:
            p = jnp.exp(s_ref[g, pl.ds(ks, kc), :] - tops[g])
            acc_ref[g] += _dot(v_ref[c, g * hd:(g + 1) * hd, :], p.astype(BF16))
            out.append(carry[g] + fold(p, jnp.sum))
        return tuple(out)

    sums = lax.fori_loop(0, nk, value_sweep, tuple(jnp.zeros((SUBLANES, wide), F32) for _ in groups))
    for g in groups:
        o = acc_ref[g] / jnp.sum(sums[g], axis=0, keepdims=True)
        for r in range(rep):
            h = g * rep + r
            o_ref[:, h * hd:(h + 1) * hd] = o[:, r * qb:(r + 1) * qb].T.astype(o_ref.dtype)


def dsa_attention(ki, k, v3, qit, qt, wit, batch, seq):
    t = batch * seq
    nqb = seq // Q_BLOCK
    kvw = ATT_KV_HEADS * ATT_HEAD_DIM
    topk = min(TOPK_MAX, seq // 4)
    idx_bits = max(1, int(math.ceil(math.log2(seq))))
    assert seq % KEY_CHUNK == 0 and KEY_CHUNK % (KEY_BITS * SUBLANES) == 0 and KEY_CHUNK % Q_BLOCK == 0
    qcol = lambda r: pl.BlockSpec((r, Q_BLOCK), lambda b, j: (0, b * nqb + j))
    return pl.pallas_call(
        functools.partial(_dsa_kernel, topk=topk, idx_bits=idx_bits),
        grid=(batch, nqb),
        in_specs=[pl.BlockSpec((seq, IDX_DIM), lambda b, j: (b, 0)),
                  pl.BlockSpec((seq, kvw), lambda b, j: (b, 0)),
                  pl.BlockSpec((seq // KEY_CHUNK, kvw, KEY_CHUNK), lambda b, j: (b, 0, 0)),
                  pl.BlockSpec((1, IDX_DIM, IDX_HEADS * Q_BLOCK), lambda b, j: (b * nqb + j, 0, 0)),
                  qcol(ATT_WIDTH), qcol(IDX_HEADS)],
        out_specs=pl.BlockSpec((Q_BLOCK, ATT_WIDTH), lambda b, j: (b * nqb + j, 0)),
        out_shape=jax.ShapeDtypeStruct((t, ATT_WIDTH), BF16),
        scratch_shapes=[pltpu.VMEM((seq, Q_BLOCK), I32),
                        pltpu.VMEM((ATT_KV_HEADS, ATT_HEAD_DIM, (ATT_HEADS // ATT_KV_HEADS) * Q_BLOCK), F32),
                        pltpu.VMEM((ATT_KV_HEADS, seq, (ATT_HEADS // ATT_KV_HEADS) * Q_BLOCK), F32),
                        pltpu.VMEM((KEY_BITS, seq // KEY_BITS, Q_BLOCK), I32)],
        compiler_params=_params(("parallel", "arbitrary")),
        name="dsa_attention",
    )(ki, k, v3, qit, qt, wit)


def _merge_kernel(x_ref, ys_ref, ya_ref, gate_ref, wbs_ref, wba_ref, wo_ref, o_ref):
    d = x_ref.shape[1]
    gates = gate_ref[...]
    merged = (_sigmoid(gates[:, :d]) * _dot(ys_ref[...], wbs_ref[...])
              + _sigmoid(gates[:, d:]) * _dot(ya_ref[...], wba_ref[...]))
    o_ref[...] = x_ref[...] + _dot(merged.astype(BF16), wo_ref[...])


def merge_branches(x, y_ssm, y_att, gates, w_bs, w_ba, w_o, tm=512):
    t, d = x.shape
    tm = min(tm, t)
    tok = lambda w: pl.BlockSpec((tm, w), lambda i: (i, 0))
    const = lambda r, w: pl.BlockSpec((r, w), lambda i: (0, 0))
    return pl.pallas_call(
        _merge_kernel,
        grid=(t // tm,),
        in_specs=[tok(d), tok(D_INNER), tok(ATT_WIDTH), tok(2 * d),
                  const(D_INNER, d), const(ATT_WIDTH, d), const(d, d)],
        out_specs=tok(d),
        out_shape=jax.ShapeDtypeStruct((t, d), F32),
        compiler_params=_params(("parallel",)),
        name="merge_branches",
    )(x, y_ssm, y_att, gates, w_bs, w_ba, w_o)


def _top_rows(s, k, rows_out):
    rows = lax.broadcasted_iota(I32, (rows_out, s.shape[1]), 0)

    def body(r, carry):
        s, top = carry
        m = jnp.max(s, axis=0, keepdims=True)
        return jnp.where(s == m, -jnp.inf, s), jnp.where(rows == r, m, top)

    _, top = lax.fori_loop(0, k, body, (s, jnp.full((rows_out, s.shape[1]), -jnp.inf, F32)))
    return top


def _peer_route_kernel(x_ref, g_ref, wqt_ref, keys_ref, ht_ref, tau_ref, e1_ref, e2_ref):
    tm = x_ref.shape[0]
    h = _rms(x_ref[...], g_ref[...])
    ht_ref[...] = h.T.astype(BF16)
    qt = lax.dot_general(wqt_ref[...], h.astype(BF16), _NT, preferred_element_type=F32).astype(BF16)
    half = PEER_QDIM // 2
    k = PEER_TOPK
    k1 = k + 1
    rows_out = -(-k1 // SUBLANES) * SUBLANES
    sub8 = lax.broadcasted_iota(I32, (SUBLANES, tm), 0)
    for hh in range(PEER_HEADS):
        s1 = _dot(keys_ref[2 * hh], qt[(2 * hh) * half:(2 * hh + 1) * half, :])
        s2 = _dot(keys_ref[2 * hh + 1], qt[(2 * hh + 1) * half:(2 * hh + 2) * half, :])
        a = _top_rows(s1, k1, rows_out)
        b = _top_rows(s2, k1, rows_out)
        assert k1 // (SUBLANES + 1) <= 1
        tiles = [a[r:r + SUBLANES] + b[0:1] for r in range(SUBLANES, rows_out, SUBLANES)]
        for i in range(SUBLANES):
            lim = k1 // (i + 1)
            for r in range(0, lim, SUBLANES):
                tile = b[r:r + SUBLANES] + a[i:i + 1]
                tiles.append(tile if lim - r >= SUBLANES else jnp.where(sub8 < lim - r, tile, -jnp.inf))
        cand = jnp.concatenate(tiles, axis=0)
        best = a[0:1] + b[0:1]

        def body(r, carry):
            cand, zsum, kth, nxt = carry
            m = jnp.max(cand, axis=0, keepdims=True)
            zsum = zsum + jnp.where(r < k, jnp.exp(m - best), 0.0)
            return jnp.where(cand == m, -jnp.inf, cand), zsum, jnp.where(r == k - 1, m, kth), m

        _, zsum, kth, nxt = lax.fori_loop(0, k1, body, (cand, jnp.zeros((1, tm), F32), best, best))
        cut = 0.5 * (kth + nxt)
        r0 = hh * N_KEYS
        tau_ref[r0:r0 + N_KEYS, :] = jnp.exp((cut - b[0:1]) - s1)
        e1_ref[r0:r0 + N_KEYS, :] = jnp.exp(s1 - a[0:1]) * (0.5 / zsum)
        e2_ref[r0:r0 + N_KEYS, :] = jnp.exp(s2 - b[0:1])


def peer_route(x, g, wq_t, keys, tm=256):
    t, d = x.shape
    tm = min(tm, t)
    rows = PEER_HEADS * N_KEYS
    col = lambda r: pl.BlockSpec((r, tm), lambda i: (0, i))
    return pl.pallas_call(
        _peer_route_kernel,
        grid=(t // tm,),
        in_specs=[pl.BlockSpec((tm, d), lambda i: (i, 0)),
                  pl.BlockSpec((1, d), lambda i: (0, 0)),
                  pl.BlockSpec(wq_t.shape, lambda i: (0, 0)),
                  pl.BlockSpec(keys.shape, lambda i: (0, 0, 0))],
        out_specs=[col(d), col(rows), col(rows), col(rows)],
        out_shape=[jax.ShapeDtypeStruct((d, t), BF16)] + [jax.ShapeDtypeStruct((rows, t), F32)] * 3,
        compiler_params=_params(("parallel",)),
        name="peer_route",
    )(x, g, wq_t, keys)


PEER_ROWS = 32


def _peer_dense_kernel(x_ref, ht_ref, tau_ref, e1_ref, e2_ref, u_ref, vt_ref, o_ref, acc_ref, hid_ref, act_ref,
                       *, ib, nblk):
    j = pl.program_id(1)
    tt = ht_ref.shape[1]
    cur = j % 2
    prev = 1 - cur

    @pl.when(j == 0)
    def _():
        acc_ref[...] = jnp.zeros_like(acc_ref)

    def second_matmul():
        acc_ref[...] += _dot(vt_ref[...], act_ref[cur])

    def first_matmul():
        hid_ref[cur] = _dot(u_ref[...], ht_ref[...])

    def gate_stage():
        blk = j - 1
        for ii in range(ib):
            i1 = blk * ib + ii
            taus = [tau_ref[pl.ds(hh * N_KEYS + i1, 1), :] for hh in range(PEER_HEADS)]
            e1s = [e1_ref[pl.ds(hh * N_KEYS + i1, 1), :] for hh in range(PEER_HEADS)]
            for lc in range(tt // LANES):
                cols = slice(lc * LANES, (lc + 1) * LANES)
                tau_b = [jnp.broadcast_to(taus[hh][:, cols], (PEER_ROWS, LANES)) for hh in range(PEER_HEADS)]
                e1_b = [jnp.broadcast_to(e1s[hh][:, cols], (PEER_ROWS, LANES)) for hh in range(PEER_HEADS)]
                for rb in range(N_KEYS // PEER_ROWS):
                    gate = jnp.zeros((PEER_ROWS, LANES), F32)
                    for hh in range(PEER_HEADS):
                        r0 = hh * N_KEYS + rb * PEER_ROWS
                        e2 = e2_ref[r0:r0 + PEER_ROWS, cols]
                        gate = gate + jnp.where(e2 >= tau_b[hh], e2, 0.0) * e1_b[hh]
                    rows = slice(ii * N_KEYS + rb * PEER_ROWS, ii * N_KEYS + (rb + 1) * PEER_ROWS)
                    hb = hid_ref[prev, rows, cols]
                    act = hb * (1.0 + lax.erf(hb * np.float32(math.sqrt(0.5))))
                    act_ref[prev, rows, cols] = (act * gate).astype(BF16)

    interior = jnp.logical_and(j >= 2, j < nblk)

    @pl.when(interior)
    def _():
        second_matmul()
        gate_stage()
        first_matmul()

    @pl.when(jnp.logical_and(jnp.logical_not(interior), j >= 2))
    def _():
        second_matmul()

    @pl.when(jnp.logical_and(jnp.logical_not(interior), jnp.logical_and(j >= 1, j <= nblk)))
    def _():
        gate_stage()

    @pl.when(jnp.logical_and(jnp.logical_not(interior), j < nblk))
    def _():
        first_matmul()

    @pl.when(j == pl.num_programs(1) - 1)
    def _():
        o_ref[...] = x_ref[...] + acc_ref[...].T


def peer_dense(x, h_t, tau, e1, e2, u, v_t, tt=512, ib=8):
    t, d = x.shape
    tt = min(tt, t)
    rows = PEER_HEADS * N_KEYS
    nblk = N_KEYS // ib
    col = lambda r: pl.BlockSpec((r, tt), lambda i, j: (0, i))
    tok = pl.BlockSpec((tt, d), lambda i, j: (i, 0))
    return pl.pallas_call(
        functools.partial(_peer_dense_kernel, ib=ib, nblk=nblk),
        grid=(t // tt, nblk + 2),
        in_specs=[tok, col(d), col(rows), col(rows), col(rows),
                  pl.BlockSpec((ib * N_KEYS, d), lambda i, j: (jnp.minimum(j, nblk - 1), 0)),
                  pl.BlockSpec((d, ib * N_KEYS), lambda i, j: (0, jnp.clip(j - 2, 0, nblk - 1)))],
        out_specs=tok,
        out_shape=jax.ShapeDtypeStruct((t, d), F32),
        scratch_shapes=[pltpu.VMEM((d, tt), F32), pltpu.VMEM((2, ib * N_KEYS, tt), F32),
                        pltpu.VMEM((2, ib * N_KEYS, tt), BF16)],
        compiler_params=_params(("parallel", "arbitrary")),
        name="peer_dense",
    )(x, h_t, tau, e1, e2, u, v_t)


def _ple_kernel(x_ref, p_ref, g_ref, wg_ref, wp_ref, o_ref):
    x = x_ref[...]
    h = _rms(x, g_ref[...]).astype(BF16)
    gate = _sigmoid(_dot(h, wg_ref[...]))
    o_ref[...] = x + gate * _dot(p_ref[...].astype(BF16), wp_ref[...])


def ple_block(x, p, g, w_gate, w_proj, tm=512):
    t, d = x.shape
    tm = min(tm, t)
    pd = p.shape[1]
    return pl.pallas_call(
        _ple_kernel,
        grid=(t // tm,),
        in_specs=[pl.BlockSpec((tm, d), lambda i: (i, 0)),
                  pl.BlockSpec((tm, pd), lambda i: (i, 0)),
                  pl.BlockSpec((1, d), lambda i: (0, 0)),
                  pl.BlockSpec((d, d), lambda i: (0, 0)),
                  pl.BlockSpec((pd, d), lambda i: (0, 0))],
        out_specs=pl.BlockSpec((tm, d), lambda i: (i, 0)),
        out_shape=jax.ShapeDtypeStruct((t, d), F32),
        compiler_params=_params(("parallel",)),
        name="ple_block",
    )(x, p, g, w_gate, w_proj)


def _split_w_in(w):
    sizes = [D_INNER, CONV_DIM, SSM_HEADS, ATT_WIDTH, ATT_KV_HEADS * ATT_HEAD_DIM,
             ATT_KV_HEADS * ATT_HEAD_DIM, IDX_HEADS * IDX_DIM, IDX_DIM, IDX_HEADS, 2 * w.shape[0]]
    out, acc = [], 0
    for s in sizes:
        out.append(w[:, acc:acc + s])
        acc += s
    return out


def kernel(x, p, positions, norm_mix, w_in, conv_w, conv_b, dt_bias, a_log, d_skip, ssm_norm, q_norm, k_norm,
           w_branch, w_out, norm_ffn, peer_wq, peer_keys, peer_u, peer_v, norm_ple, w_ple_gate, w_ple_proj):
    batch, seq, d = x.shape
    t = batch * seq
    depth = w_in.shape[0]
    tables = rope_tables_t(positions)
    xf = x.reshape(t, d)
    for i in range(depth):
        wz, wxbc, wdt, wq, wk, wv, wqi, wki, wwi, wg = _split_w_in(w_in[i])
        w_dt = jnp.pad(wdt, ((0, 0), (0, LANES - SSM_HEADS))).astype(BF16)
        w_att_t = jnp.transpose(jnp.concatenate([wq, wk, wv, wqi, wki, wwi], axis=1)).astype(BF16)
        g_mix = norm_mix[i].reshape(1, d)

        zx = norm_matmul(xf, g_mix, jnp.concatenate([wz, wxbc], axis=1).astype(BF16))
        gates_dt = norm_matmul(xf, g_mix, jnp.concatenate([wg.astype(BF16), w_dt], axis=1))
        proj_t = norm_matmul_t(xf, g_mix, w_att_t)

        y_ssm = ssm_branch(zx, gates_dt, conv_w[i], conv_b[i], dt_bias[i], a_log[i], d_skip[i],
                           ssm_norm[i], batch, seq)
        qt, k, v3, qit, ki, wit = dsa_prep(proj_t, tables, q_norm[i], k_norm[i])
        y_att = dsa_attention(ki, k, v3, qit, qt, wit, batch, seq)
        xf = merge_branches(xf, y_ssm, y_att, gates_dt, w_branch[i, :D_INNER].astype(BF16),
                            w_branch[i, D_INNER:].astype(BF16), w_out[i].astype(BF16))

        keys = peer_keys[i].reshape(PEER_HEADS * 2, N_KEYS, PEER_QDIM // 2).astype(BF16)
        h_t, tau, e1, e2 = peer_route(xf, norm_ffn[i].reshape(1, d), jnp.transpose(peer_wq[i]).astype(BF16), keys)
        xf = peer_dense(xf, h_t, tau, e1, e2, peer_u[i].astype(BF16), jnp.transpose(peer_v[i]).astype(BF16))

        xf = ple_block(xf, p[i].reshape(t, -1), norm_ple[i].reshape(1, d),
                       w_ple_gate[i].astype(BF16), w_ple_proj[i].astype(BF16))
    return xf.reshape(batch, seq, d)
```
